```python
import math
import jax, jax.numpy as jnp
from jax import lax
import numpy as np

D_MODEL = 1024
BATCH = 2
SEQ = 8192
DEPTH = 4
DEC_BATCH = 128
DEC_SEQ = 4
PAST_LEN = 2048
PAGE_SIZE = 128

HEAD_DIM = 64
N_RET_HEADS = 8
RET_DK = HEAD_DIM
RET_DV = HEAD_DIM
RET_WIDTH = N_RET_HEADS * RET_DV
N_DIFF_HEADS = 4
DIFF_DV = 2 * HEAD_DIM
DIFF_WIDTH = N_DIFF_HEADS * DIFF_DV
MIX_WIDTH = RET_WIDTH + DIFF_WIDTH
W_RQ = N_RET_HEADS * RET_DK
W_RK = N_RET_HEADS * RET_DK
W_RV = N_RET_HEADS * RET_DV
W_RG = RET_WIDTH
W_DQ = N_DIFF_HEADS * 2 * HEAD_DIM
W_DK = N_DIFF_HEADS * 2 * HEAD_DIM
W_DV = DIFF_WIDTH
IN_SPLITS = (W_RQ, W_RQ + W_RK, W_RQ + W_RK + W_RV, W_RQ + W_RK + W_RV + W_RG,
             W_RQ + W_RK + W_RV + W_RG + W_DQ, W_RQ + W_RK + W_RV + W_RG + W_DQ + W_DK)
IN_WIDTH = W_RQ + W_RK + W_RV + W_RG + W_DQ + W_DK + W_DV
RET_CHUNK = 128
Q_BLOCK = 128
N_MEM = 256
N_MEM_HEADS = 4
MEM_HEAD_DIM = D_MODEL // N_MEM_HEADS
MEM_WIDTH = N_MEM_HEADS * MEM_HEAD_DIM
D_FF = 2816
CONV_WIDTH = 3
ROPE_THETA = 10000.0
EPS = 1e-6
NEG_BIG = -1e30

kernel_name = "hybrid_retention_diffattn_decoder_step"


def rms_norm(x, g):
    xf = x.astype(jnp.float32)
    y = xf * lax.rsqrt(jnp.mean(xf * xf, axis=-1, keepdims=True) + EPS)
    return (y * g.astype(jnp.float32)).astype(x.dtype)


def rope(x, pos):
    hd = x.shape[-1]
    inv = 1.0 / (ROPE_THETA ** (jnp.arange(0, hd, 2, dtype=jnp.float32) / hd))
    ang = pos.astype(jnp.float32)[:, None] * inv[None, :]
    ang = jnp.concatenate([ang, ang], axis=-1)
    shape = (pos.shape[0],) + (1,) * (x.ndim - 3) + (hd,)
    cos = jnp.cos(ang).reshape(shape).astype(x.dtype)
    sin = jnp.sin(ang).reshape(shape).astype(x.dtype)
    x1, x2 = jnp.split(x, 2, axis=-1)
    return x * cos + jnp.concatenate([-x2, x1], axis=-1) * sin


def retention_log_decay():
    return jnp.log1p(-jnp.exp2(-5.0 - jnp.arange(N_RET_HEADS, dtype=jnp.float32)))


def retention_chunk(q, k, v, s, log_g):
    L = q.shape[1]
    dt = q.dtype
    n = jnp.arange(L, dtype=jnp.float32)
    diff = n[:, None] - n[None, :]
    dec = jnp.where((diff >= 0)[None], jnp.exp(log_g[:, None, None] * jnp.maximum(diff, 0.0)[None]), 0.0)
    read_w = jnp.exp(log_g[:, None] * (n + 1.0)[None]).T
    write_w = jnp.exp(log_g[:, None] * (L - 1.0 - n)[None]).T
    chunk_decay = jnp.exp(log_g * L)
    scores = jnp.einsum('blhd,bmhd->bhlm', q, k) * dec.astype(dt)[None]
    o = jnp.einsum('bhlm,bmhe->blhe', scores, v)
    o = o + jnp.einsum('blhd,bhde->blhe', q, s) * read_w.astype(dt)[None, :, :, None]
    s_new = (s * chunk_decay.astype(dt)[None, :, None, None]
             + jnp.einsum('blhd,blhe->bhde', k * write_w.astype(dt)[None, :, :, None], v))
    return o, s_new


def retention_prompt(q, k, v, log_g):
    B, S, H, dk = q.shape
    nc = S // RET_CHUNK

    def to_chunks(t):
        return t.reshape(B, nc, RET_CHUNK, H, t.shape[-1]).swapaxes(0, 1)

    s0 = jnp.zeros((B, H, dk, v.shape[-1]), q.dtype)

    def step(s, inp):
        qc, kc, vc = inp
        o, s = retention_chunk(qc, kc, vc, s, log_g)
        return s, o

    s_fin, o = lax.scan(step, s0, (to_chunks(q), to_chunks(k), to_chunks(v)))
    return o.swapaxes(0, 1).reshape(B, S, H, v.shape[-1]), s_fin


def diff_attend(q, k, v, q_pos, k_pos, lam):
    s = jnp.einsum('bqhcd,bkhcd->bhcqk', q, k).astype(jnp.float32) * (HEAD_DIM ** -0.5)
    mask = k_pos[None, :] <= q_pos[:, None]
    p = jax.nn.softmax(jnp.where(mask, s, NEG_BIG), axis=-1)
    a = (p[:, :, 0] - lam * p[:, :, 1]).astype(v.dtype)
    return jnp.einsum('bhqk,bkhe->bqhe', a, v)


def diff_attn_prompt(q, k, v, lam):
    B, S = q.shape[:2]
    k_pos = jnp.arange(S, dtype=jnp.int32)

    def blk(i):
        start = i * Q_BLOCK
        qb = lax.dynamic_slice_in_dim(q, start, Q_BLOCK, axis=1)
        q_pos = start + jnp.arange(Q_BLOCK, dtype=jnp.int32)
        return diff_attend(qb, k, v, q_pos, k_pos, lam)

    o = lax.map(blk, jnp.arange(S // Q_BLOCK, dtype=jnp.int32))
    return o.swapaxes(0, 1).reshape(B, S, N_DIFF_HEADS, DIFF_DV)


def diff_lambda(lq1, lk1, lq2, lk2, layer):
    lam_init = 0.8 - 0.6 * math.exp(-0.3 * layer)
    f = jnp.float32
    lam = (jnp.exp(jnp.sum(lq1.astype(f) * lk1.astype(f)))
           - jnp.exp(jnp.sum(lq2.astype(f) * lk2.astype(f))) + lam_init)
    return lam, lam_init


def mixer_project(h, pos, w_in):
    B, S, _ = h.shape
    z = h @ w_in
    rq, rk, rv, rg, dq, dk, dv = jnp.split(z, IN_SPLITS, axis=-1)
    rq = rope(rq.reshape(B, S, N_RET_HEADS, RET_DK), pos)
    rk = rope(rk.reshape(B, S, N_RET_HEADS, RET_DK), pos) * (RET_DK ** -0.5)
    rv = rv.reshape(B, S, N_RET_HEADS, RET_DV)
    dq = rope(dq.reshape(B, S, N_DIFF_HEADS, 2, HEAD_DIM), pos)
    dk = rope(dk.reshape(B, S, N_DIFF_HEADS, 2, HEAD_DIM), pos)
    dv = dv.reshape(B, S, N_DIFF_HEADS, DIFF_DV)
    return rq, rk, rv, rg, dq, dk, dv


def mixer_merge(ro, rg, do, ret_g, diff_g, lam_init, w_out):
    B, S = ro.shape[:2]
    rf = ro.astype(jnp.float32)
    mu = jnp.mean(rf, axis=-1, keepdims=True)
    var = jnp.mean(jnp.square(rf - mu), axis=-1, keepdims=True)
    ro = ((rf - mu) * lax.rsqrt(var + EPS) * ret_g.astype(jnp.float32)).astype(rg.dtype)
    ro = jax.nn.silu(rg) * ro.reshape(B, S, RET_WIDTH)
    do = (rms_norm(do, diff_g) * (1.0 - lam_init)).reshape(B, S, DIFF_WIDTH)
    return jnp.concatenate([ro, do], axis=-1) @ w_out


def mem_kv(mem, g, w_mk, w_mv):
    B, M, _ = mem.shape
    m = rms_norm(mem, g)
    return ((m @ w_mk).reshape(B, M, N_MEM_HEADS, MEM_HEAD_DIM),
            (m @ w_mv).reshape(B, M, N_MEM_HEADS, MEM_HEAD_DIM))


def mem_attend(h, mk, mv, w_mq, w_mo):
    B, S, _ = h.shape
    q = (h @ w_mq).reshape(B, S, N_MEM_HEADS, MEM_HEAD_DIM)
    s = jnp.einsum('bqhd,bmhd->bhqm', q, mk.astype(h.dtype)).astype(jnp.float32) * (MEM_HEAD_DIM ** -0.5)
    p = jax.nn.softmax(s, axis=-1).astype(h.dtype)
    o = jnp.einsum('bhqm,bmhd->bqhd', p, mv.astype(h.dtype)).reshape(B, S, MEM_WIDTH)
    return o @ w_mo


def conv_ffn(h, hist, w_up, conv_w, conv_b, w_down):
    S = h.shape[1]
    u = h @ w_up
    ue = jnp.concatenate([hist.astype(u.dtype), u], axis=1)
    c = conv_b
    for j in range(CONV_WIDTH):
        c = c + conv_w[j] * ue[:, j:j + S]
    gate, val = jnp.split(c, 2, axis=-1)
    return (jax.nn.silu(gate) * val) @ w_down, ue[:, -(CONV_WIDTH - 1):]


def setup_inputs(seed: int = 0) -> dict:
    key = jax.random.key(seed)
    ks = iter(jax.random.split(key, 40))
    f32 = jnp.float32
    n_pages = PAST_LEN // PAGE_SIZE
    n_pool = (5 * DEC_BATCH * n_pages) // 4

    def nrm(shape, scale=1.0):
        return jax.random.normal(next(ks), shape, f32) * scale

    def gain(shape):
        return 1.0 + nrm(shape, 0.01)

    inp = {}
    inp['x_prompt'] = nrm((BATCH, SEQ, D_MODEL))
    inp['x_sample'] = nrm((DEC_BATCH, DEC_SEQ, D_MODEL))
    inp['mem_prompt'] = nrm((BATCH, N_MEM, D_MODEL))
    inp['state_ret'] = nrm((DEPTH, DEC_BATCH, N_RET_HEADS, RET_DK, RET_DV))
    inp['cache_diff_k'] = nrm((DEPTH, n_pool, PAGE_SIZE, N_DIFF_HEADS, 2 * HEAD_DIM))
    inp['cache_diff_v'] = nrm((DEPTH, n_pool, PAGE_SIZE, N_DIFF_HEADS, DIFF_DV))
    perm = jax.random.permutation(next(ks), n_pool)
    inp['page_table'] = perm[:DEC_BATCH * n_pages].reshape(DEC_BATCH, n_pages).astype(jnp.int32)
    inp['cache_mem_k'] = nrm((DEPTH, DEC_BATCH, N_MEM, N_MEM_HEADS, MEM_HEAD_DIM))
    inp['cache_mem_v'] = nrm((DEPTH, DEC_BATCH, N_MEM, N_MEM_HEADS, MEM_HEAD_DIM))
    inp['state_conv'] = nrm((DEPTH, DEC_BATCH, CONV_WIDTH - 1, 2 * D_FF))
    inp['g_mix_pre'] = gain((DEPTH, D_MODEL))
    inp['g_mix_post'] = gain((DEPTH, D_MODEL))
    inp['w_in'] = nrm((DEPTH, D_MODEL, IN_WIDTH), D_MODEL ** -0.5)
    inp['ret_norm_g'] = gain((DEPTH, N_RET_HEADS, RET_DV))
    inp['lam_q1'] = nrm((DEPTH, HEAD_DIM), 0.1)
    inp['lam_k1'] = nrm((DEPTH, HEAD_DIM), 0.1)
    inp['lam_q2'] = nrm((DEPTH, HEAD_DIM), 0.1)
    inp['lam_k2'] = nrm((DEPTH, HEAD_DIM), 0.1)
    inp['diff_norm_g'] = gain((DEPTH, N_DIFF_HEADS, DIFF_DV))
    inp['w_out'] = nrm((DEPTH, MIX_WIDTH, D_MODEL), MIX_WIDTH ** -0.5)
    inp['g_mem_pre'] = gain((DEPTH, D_MODEL))
    inp['g_mem_post'] = gain((DEPTH, D_MODEL))
    inp['g_mem_kv'] = gain((DEPTH, D_MODEL))
    inp['w_mq'] = nrm((DEPTH, D_MODEL, MEM_WIDTH), D_MODEL ** -0.5)
    inp['w_mk'] = nrm((DEPTH, D_MODEL, MEM_WIDTH), D_MODEL ** -0.5)
    inp['w_mv'] = nrm((DEPTH, D_MODEL, MEM_WIDTH), D_MODEL ** -0.5)
    inp['w_mo'] = nrm((DEPTH, MEM_WIDTH, D_MODEL), MEM_WIDTH ** -0.5)
    inp['g_ffn_pre'] = gain((DEPTH, D_MODEL))
    inp['g_ffn_post'] = gain((DEPTH, D_MODEL))
    inp['w_up'] = nrm((DEPTH, D_MODEL, 2 * D_FF), D_MODEL ** -0.5)
    inp['conv_w'] = nrm((DEPTH, CONV_WIDTH, 2 * D_FF), CONV_WIDTH ** -0.5)
    inp['conv_b'] = nrm((DEPTH, 2 * D_FF), 0.01)
    inp['w_down'] = nrm((DEPTH, D_FF, D_MODEL), D_FF ** -0.5)
    return inp


def reference(x_prompt, x_sample, mem_prompt, state_ret, cache_diff_k, cache_diff_v, page_table,
              cache_mem_k, cache_mem_v, state_conv, g_mix_pre, g_mix_post, w_in, ret_norm_g,
              lam_q1, lam_k1, lam_q2, lam_k2, diff_norm_g, w_out, g_mem_pre, g_mem_post, g_mem_kv,
              w_mq, w_mk, w_mv, w_mo, g_ffn_pre, g_ffn_post, w_up, conv_w, conv_b, w_down):
    B, S, _ = x_prompt.shape
    DB, DS, _ = x_sample.shape
    past = page_table.shape[1] * cache_diff_k.shape[2]
    pos_p = jnp.arange(S, dtype=jnp.int32)
    pos_s = past + jnp.arange(DS, dtype=jnp.int32)
    kpos_s = jnp.arange(past + DS, dtype=jnp.int32)
    log_g = retention_log_decay()

    xp, xs = x_prompt, x_sample
    p_ret, p_dk, p_dv, p_mk, p_mv, p_conv = [], [], [], [], [], []
    s_ret, s_dk, s_dv, s_conv = [], [], [], []

    for l in range(DEPTH):
        lam, lam_init = diff_lambda(lam_q1[l], lam_k1[l], lam_q2[l], lam_k2[l], l)

        hp = rms_norm(xp, g_mix_pre[l])
        rq, rk, rv, rg, dq, dk, dv = mixer_project(hp, pos_p, w_in[l])
        ro, s_fin = retention_prompt(rq, rk, rv, log_g)
        do = diff_attn_prompt(dq, dk, dv, lam)
        xp = xp + rms_norm(mixer_merge(ro, rg, do, ret_norm_g[l], diff_norm_g[l], lam_init, w_out[l]), g_mix_post[l])
        p_ret.append(s_fin)
        p_dk.append(dk.reshape(B, S, N_DIFF_HEADS, 2 * HEAD_DIM))
        p_dv.append(dv)

        hs = rms_norm(xs, g_mix_pre[l])
        rq, rk, rv, rg, dq, dk, dv = mixer_project(hs, pos_s, w_in[l])
        ro, s_new = retention_chunk(rq, rk, rv, state_ret[l].astype(rq.dtype), log_g)
        kp = cache_diff_k[l, page_table].reshape(DB, past, N_DIFF_HEADS, 2, HEAD_DIM).astype(dk.dtype)
        vp = cache_diff_v[l, page_table].reshape(DB, past, N_DIFF_HEADS, DIFF_DV).astype(dv.dtype)
        do = diff_attend(dq, jnp.concatenate([kp, dk], axis=1), jnp.concatenate([vp, dv], axis=1),
                         pos_s, kpos_s, lam)
        xs = xs + rms_norm(mixer_merge(ro, rg, do, ret_norm_g[l], diff_norm_g[l], lam_init, w_out[l]), g_mix_post[l])
        s_ret.append(s_new)
        s_dk.append(dk.reshape(DB, DS, N_DIFF_HEADS, 2 * HEAD_DIM))
        s_dv.append(dv)

        mk, mv = mem_kv(mem_prompt, g_mem_kv[l], w_mk[l], w_mv[l])
        xp = xp + rms_norm(mem_attend(rms_norm(xp, g_mem_pre[l]), mk, mv, w_mq[l], w_mo[l]), g_mem_post[l])
        xs = xs + rms_norm(mem_attend(rms_norm(xs, g_mem_pre[l]), cache_mem_k[l], cache_mem_v[l],
                                      w_mq[l], w_mo[l]), g_mem_post[l])
        p_mk.append(mk)
        p_mv.append(mv)

        fp, cp = conv_ffn(rms_norm(xp, g_ffn_pre[l]), jnp.zeros((B, CONV_WIDTH - 1, 2 * D_FF), xp.dtype),
                          w_up[l], conv_w[l], conv_b[l], w_down[l])
        xp = xp + rms_norm(fp, g_ffn_post[l])
        fs, cs = conv_ffn(rms_norm(xs, g_ffn_pre[l]), state_conv[l], w_up[l], conv_w[l], conv_b[l], w_down[l])
        xs = xs + rms_norm(fs, g_ffn_post[l])
        p_conv.append(cp)
        s_conv.append(cs)

    return (xp, xs, jnp.stack(p_ret), jnp.stack(p_dk), jnp.stack(p_dv), jnp.stack(p_mk), jnp.stack(p_mv),
            jnp.stack(p_conv), jnp.stack(s_ret), jnp.stack(s_dk), jnp.stack(s_dv), jnp.stack(s_conv))
```

```python
import functools
import math

import jax
import jax.numpy as jnp
from jax import lax
from jax.experimental import pallas as pl
from jax.experimental.pallas import tpu as pltpu

F32 = jnp.float32
BF16 = jnp.bfloat16

HEAD_DIM = 64
N_RET_HEADS = 8
N_DIFF_HEADS = 4
DIFF_DV = 2 * HEAD_DIM
SECTION = 512
N_SECTIONS = 7
N_MEM_HEADS = 4
CONV_WIDTH = 3
ROPE_THETA = 10000.0
EPS = 1e-6
NEG_BIG = -1e30
RET_CHUNK = 128

LANES = 128
V7X_VMEM_BYTES = 64 * 1024 * 1024
VMEM_LIMIT = (V7X_VMEM_BYTES * 3) // 4


def _cparams(semantics):
    return pltpu.CompilerParams(dimension_semantics=semantics, vmem_limit_bytes=VMEM_LIMIT)


def _rms(x, g):
    return x * lax.rsqrt(jnp.mean(x * x, axis=-1, keepdims=True) + EPS) * g


def _dot(a, b):
    return jnp.dot(a, b, preferred_element_type=F32)


def _dot_nt(a, b):
    return lax.dot_general(a, b, (((1,), (1,)), ((), ())), preferred_element_type=F32)


def _silu(x):
    return x * (1.0 / (1.0 + jnp.exp(-x)))


def _proj_kernel(x_ref, g_ref, w_ref, cos_ref, sa_ref, sb_ref, *out_and_scratch, rope_sections, k_section,
                 n_main):
    outs, xn_ref = out_and_scratch[:-1], out_and_scratch[-1]
    j = pl.program_id(1)

    @pl.when(j == 0)
    def _():
        xn_ref[...] = _rms(x_ref[...], g_ref[...]).astype(BF16)

    acc = _dot(xn_ref[...], w_ref[...])
    n_sections = n_main + len(outs) - 1
    for sec in range(n_sections):
        o_ref = outs[0] if sec < n_main else outs[sec - n_main + 1]

        @pl.when(j == sec)
        def _(o_ref=o_ref, sec=sec):
            if sec in rope_sections:
                scale = HEAD_DIM ** -0.5 if sec == k_section else 1.0
                c, sa, sb = cos_ref[...], sa_ref[...], sb_ref[...]
                for t in range(acc.shape[1] // LANES):
                    a = acc[:, t * LANES:(t + 1) * LANES]
                    r = a * c + pltpu.roll(a, LANES - HEAD_DIM // 2, 1) * sa + pltpu.roll(a, HEAD_DIM // 2, 1) * sb
                    o_ref[:, t * LANES:(t + 1) * LANES] = r * scale if scale != 1.0 else r
            else:
                o_ref[...] = acc


def _project(x, g, w_bf, tabs, *, tm, tn, rope_sections=(), k_section=-1, n_main=None):
    m, d = x.shape
    n = w_bf.shape[1]
    n_sec = n // tn
    n_main = n_sec if n_main is None else n_main
    cos, sa, sb = tabs
    tab_blocks = cos.shape[0] // tm
    tab_spec = pl.BlockSpec((tm, LANES), lambda i, j: (i % tab_blocks, 0))
    out_shape = [jax.ShapeDtypeStruct((m, n_main * tn), F32)]
    out_specs = [pl.BlockSpec((tm, tn), lambda i, j: (i, jnp.minimum(j, n_main - 1)))]
    for _ in range(n_sec - n_main):
        out_shape.append(jax.ShapeDtypeStruct((m, tn), F32))
        out_specs.append(pl.BlockSpec((tm, tn), lambda i, j: (i, 0)))
    kern = functools.partial(_proj_kernel, rope_sections=tuple(rope_sections), k_section=k_section, n_main=n_main)
    return pl.pallas_call(
        kern,
        grid=(m // tm, n_sec),
        in_specs=[pl.BlockSpec((tm, d), lambda i, j: (i, 0)),
                  pl.BlockSpec((1, d), lambda i, j: (0, 0)),
                  pl.BlockSpec((d, tn), lambda i, j: (0, j)),
                  tab_spec, tab_spec, tab_spec],
        out_specs=out_specs,
        out_shape=out_shape,
        scratch_shapes=[pltpu.VMEM((tm, d), BF16)],
        compiler_params=_cparams(("parallel", "arbitrary")),
        name="project",
    )(x, g.reshape(1, d), w_bf, cos, sa, sb)


def _ret_prompt_kernel(q_ref, k_ref, v_ref, dec_ref, rw_ref, ww_ref, cd_ref, g_ref, o_ref, sfin_ref, s_ref, *,
                       n_chunks):
    r = pl.program_id(1)

    @pl.when(r == 0)
    def _():
        s_ref[...] = jnp.zeros_like(s_ref)

    def chunk(c, carry):
        row = pl.multiple_of(c * RET_CHUNK, RET_CHUNK)
        rows = pl.ds(row, RET_CHUNK)
        for h in range(N_RET_HEADS):
            cols = slice(h * HEAD_DIM, (h + 1) * HEAD_DIM)
            q = q_ref[rows, cols]
            k = k_ref[rows, cols]
            vb = v_ref[rows, cols].astype(BF16)
            qb = q.astype(BF16)
            s = s_ref[h]
            scores = _dot_nt(qb, k.astype(BF16)) * dec_ref[h]
            o = _dot(scores.astype(BF16), vb) + _dot(qb, s.astype(BF16)) * rw_ref[h]
            kw_t = (k * ww_ref[h]).T.astype(BF16)
            s_ref[h] = s * cd_ref[h] + _dot(kw_t, vb)
            mu = jnp.mean(o, axis=-1, keepdims=True)
            var = jnp.mean(jnp.square(o - mu), axis=-1, keepdims=True)
            o_ref[rows, cols] = (o - mu) * lax.rsqrt(var + EPS) * g_ref[h]
        return carry

    lax.fori_loop(0, n_chunks, chunk, 0)

    @pl.when(r == pl.num_programs(1) - 1)
    def _():
        sfin_ref[0] = s_ref[...]


def _retention_tables(length):
    log_g = jnp.log1p(-jnp.exp2(-5.0 - jnp.arange(N_RET_HEADS, dtype=F32)))
    n = jnp.arange(length, dtype=F32)
    diff = n[:, None] - n[None, :]
    dec = jnp.where((diff >= 0)[None], jnp.exp(log_g[:, None, None] * jnp.maximum(diff, 0.0)[None]), 0.0)
    read_w = jnp.exp(log_g[:, None] * (n + 1.0)[None])
    write_w = jnp.exp(log_g[:, None] * (length - 1.0 - n)[None])
    chunk_decay = jnp.exp(log_g * length)
    return dec, read_w, write_w, chunk_decay


def _retention_prompt(z, ret_g, batch, seq, *, rows_per_step):
    m = z.shape[0]
    nr = seq // rows_per_step
    dec, read_w, write_w, chunk_decay = _retention_tables(RET_CHUNK)
    rw = jnp.broadcast_to(read_w[:, :, None], (N_RET_HEADS, RET_CHUNK, HEAD_DIM))
    ww = jnp.broadcast_to(write_w[:, :, None], (N_RET_HEADS, RET_CHUNK, HEAD_DIM))
    cd = jnp.broadcast_to(chunk_decay[:, None, None], (N_RET_HEADS, HEAD_DIM, HEAD_DIM))
    full3 = lambda shape: pl.BlockSpec(shape, lambda b, r: (0, 0, 0))
    sec = lambda c: pl.BlockSpec((rows_per_step, SECTION), lambda b, r: (b * nr + r, c))
    return pl.pallas_call(
        functools.partial(_ret_prompt_kernel, n_chunks=rows_per_step // RET_CHUNK),
        grid=(batch, nr),
        in_specs=[sec(0), sec(1), sec(2),
                  full3((N_RET_HEADS, RET_CHUNK, RET_CHUNK)),
                  full3((N_RET_HEADS, RET_CHUNK, HEAD_DIM)),
                  full3((N_RET_HEADS, RET_CHUNK, HEAD_DIM)),
                  full3((N_RET_HEADS, HEAD_DIM, HEAD_DIM)),
                  full3((N_RET_HEADS, 1, HEAD_DIM))],
        out_specs=[pl.BlockSpec((rows_per_step, SECTION), lambda b, r: (b * nr + r, 0)),
                   pl.BlockSpec((1, N_RET_HEADS, HEAD_DIM, HEAD_DIM), lambda b, r: (b, 0, 0, 0))],
        out_shape=[jax.ShapeDtypeStruct((m, SECTION), F32),
                   jax.ShapeDtypeStruct((batch, N_RET_HEADS, HEAD_DIM, HEAD_DIM), F32)],
        scratch_shapes=[pltpu.VMEM((N_RET_HEADS, HEAD_DIM, HEAD_DIM), F32)],
        compiler_params=_cparams(("parallel", "arbitrary")),
        name="retention_prompt",
    )(z, z, z, dec, rw, ww, cd, ret_g.reshape(N_RET_HEADS, 1, HEAD_DIM))


def _ret_sample_kernel(q_ref, k_ref, v_ref, s_ref, dec_ref, rw_ref, ww_ref, cd_ref, g_ref, o_ref, snew_ref):
    n = q_ref.shape[0] * q_ref.shape[1]
    length = q_ref.shape[2]
    q = q_ref[...].reshape(n, length, HEAD_DIM)
    k = k_ref[...].reshape(n, length, HEAD_DIM)
    vb = v_ref[...].reshape(n, length, HEAD_DIM).astype(BF16)
    s = s_ref[...].reshape(n, HEAD_DIM, HEAD_DIM)
    qb = q.astype(BF16)
    scores = jnp.einsum('nqd,nkd->nqk', qb, k.astype(BF16), preferred_element_type=F32) * dec_ref[...]
    o = jnp.einsum('nqk,nke->nqe', scores.astype(BF16), vb, preferred_element_type=F32)
    o = o + jnp.einsum('nqd,nde->nqe', qb, s.astype(BF16), preferred_element_type=F32) * rw_ref[...]
    kw = (k * ww_ref[...]).astype(BF16)
    s_new = s * cd_ref[...] + jnp.einsum('nld,nle->nde', kw, vb, preferred_element_type=F32)
    mu = jnp.mean(o, axis=-1, keepdims=True)
    var = jnp.mean(jnp.square(o - mu), axis=-1, keepdims=True)
    o_ref[...] = ((o - mu) * lax.rsqrt(var + EPS) * g_ref[...]).reshape(o_ref.shape)
    snew_ref[...] = s_new.reshape(snew_ref.shape)


def _retention_sample(q, k, v, state, ret_g, n_tokens, *, bb):
    db, nh, length, _ = q.shape
    dec, read_w, write_w, chunk_decay = _retention_tables(n_tokens)
    pad = length - n_tokens
    dec = jnp.pad(dec, ((0, 0), (0, pad), (0, pad)))
    read_w = jnp.pad(read_w, ((0, 0), (0, pad)))
    write_w = jnp.pad(write_w, ((0, 0), (0, pad)))
    n = bb * nh
    tile = lambda t: jnp.tile(t, (bb,) + (1,) * (t.ndim - 1))
    dec_t = tile(dec)
    rw_t = tile(jnp.broadcast_to(read_w[:, :, None], (nh, length, HEAD_DIM)))
    ww_t = tile(jnp.broadcast_to(write_w[:, :, None], (nh, length, HEAD_DIM)))
    cd_t = tile(jnp.broadcast_to(chunk_decay[:, None, None], (nh, HEAD_DIM, HEAD_DIM)))
    g_t = tile(jnp.broadcast_to(ret_g[:, None, :], (nh, length, HEAD_DIM)))
    blk = pl.BlockSpec((bb, nh, length, HEAD_DIM), lambda i: (i, 0, 0, 0))
    sblk = pl.BlockSpec((bb, nh, HEAD_DIM, HEAD_DIM), lambda i: (i, 0, 0, 0))
    full = lambda a: pl.BlockSpec(a.shape, lambda i: (0,) * a.ndim)
    return pl.pallas_call(
        _ret_sample_kernel,
        grid=(db // bb,),
        in_specs=[blk, blk, blk, sblk, full(dec_t), full(rw_t), full(ww_t), full(cd_t), full(g_t)],
        out_specs=[blk, sblk],
        out_shape=[jax.ShapeDtypeStruct(q.shape, F32), jax.ShapeDtypeStruct(state.shape, F32)],
        compiler_params=_cparams(("parallel",)),
        name="retention_sample",
    )(q, k, v, state, dec_t, rw_t, ww_t, cd_t, g_t)


def _lambda(lq1_ref, lk1_ref, lq2_ref, lk2_ref, lam_init):
    e1 = jnp.exp(jnp.sum(lq1_ref[...] * lk1_ref[...], axis=-1, keepdims=True))
    e2 = jnp.exp(jnp.sum(lq2_ref[...] * lk2_ref[...], axis=-1, keepdims=True))
    return e1 - e2 + lam_init


def _diff_prompt_kernel(qi_ref, ki_ref, q_ref, k_ref, v_ref, lq1_ref, lk1_ref, lq2_ref, lk2_ref, g_ref, o_ref,
                        m_ref, l_ref, acc_ref, *, lam_init):
    p = pl.program_id(2)
    qi = qi_ref[p]
    ki = ki_ref[p]

    @pl.when(ki == 0)
    def _():
        m_ref[...] = jnp.full_like(m_ref, NEG_BIG)
        l_ref[...] = jnp.zeros_like(l_ref)
        acc_ref[...] = jnp.zeros_like(acc_ref)

    def step(masked):
        q = q_ref[...] * (HEAD_DIM ** -0.5)
        k = k_ref[...]
        vb = v_ref[...].astype(BF16)
        for c in range(2):
            cols = slice(c * HEAD_DIM, (c + 1) * HEAD_DIM)
            s = _dot_nt(q[:, cols].astype(BF16), k[:, cols].astype(BF16))
            if masked:
                row = lax.broadcasted_iota(jnp.int32, s.shape, 0)
                col = lax.broadcasted_iota(jnp.int32, s.shape, 1)
                s = jnp.where(col <= row, s, NEG_BIG)
            m_old = m_ref[c]
            m_new = jnp.maximum(m_old, jnp.max(s, axis=-1, keepdims=True))
            alpha = jnp.exp(m_old - m_new)
            e = jnp.exp(s - m_new)
            l_ref[c] = alpha * l_ref[c] + jnp.sum(e, axis=-1, keepdims=True)
            acc_ref[c] = alpha * acc_ref[c] + _dot(e.astype(BF16), vb)
            m_ref[c] = m_new

    @pl.when(ki < qi)
    def _():
        step(False)

    @pl.when(ki == qi)
    def _():
        step(True)
        lam = _lambda(lq1_ref, lk1_ref, lq2_ref, lk2_ref, lam_init)
        o = acc_ref[0] / l_ref[0] - lam * (acc_ref[1] / l_ref[1])
        o_ref[...] = _rms(o, g_ref[0]) * (1.0 - lam_init)


def _diff_prompt(z, dk, dv, lam_params, diff_g, lam_init, batch, seq, *, tile):
    m = z.shape[0]
    nt = seq // tile
    pairs = [(qi, ki) for qi in range(nt) for ki in range(qi + 1)]
    qi_tab = jnp.asarray([p[0] for p in pairs], jnp.int32)
    ki_tab = jnp.asarray([p[1] for p in pairs], jnp.int32)
    q_col0 = 4 * SECTION // LANES
    k_col0 = 0
    v_col0 = 0
    lam_spec = pl.BlockSpec((1, HEAD_DIM), lambda b, h, p, qi, ki: (0, 0))
    grid_spec = pltpu.PrefetchScalarGridSpec(
        num_scalar_prefetch=2,
        grid=(batch, N_DIFF_HEADS, len(pairs)),
        in_specs=[pl.BlockSpec((tile, LANES), lambda b, h, p, qi, ki: (b * nt + qi[p], q_col0 + h)),
                  pl.BlockSpec((tile, LANES), lambda b, h, p, qi, ki: (b * nt + ki[p], k_col0 + h)),
                  pl.BlockSpec((tile, LANES), lambda b, h, p, qi, ki: (b * nt + ki[p], v_col0 + h)),
                  lam_spec, lam_spec, lam_spec, lam_spec,
                  pl.BlockSpec((1, 1, DIFF_DV), lambda b, h, p, qi, ki: (h, 0, 0))],
        out_specs=pl.BlockSpec((tile, LANES), lambda b, h, p, qi, ki: (b * nt + qi[p], h)),
        scratch_shapes=[pltpu.VMEM((2, tile, 1), F32), pltpu.VMEM((2, tile, 1), F32),
                        pltpu.VMEM((2, tile, DIFF_DV), F32)],
    )
    return pl.pallas_call(
        functools.partial(_diff_prompt_kernel, lam_init=lam_init),
        grid_spec=grid_spec,
        out_shape=jax.ShapeDtypeStruct((m, N_DIFF_HEADS * DIFF_DV), F32),
        compiler_params=_cparams(("parallel", "parallel", "arbitrary")),
        name="diff_attn_prompt",
    )(qi_tab, ki_tab, z, dk, dv, *lam_params, diff_g.reshape(N_DIFF_HEADS, 1, DIFF_DV))


def _diff_sample_kernel(pt_ref, q_ref, kn_ref, vn_ref, *rest, n_pages, lam_init):
    k_pages = rest[:n_pages]
    v_pages = rest[n_pages:2 * n_pages]
    g_ref, lq1_ref, lk1_ref, lq2_ref, lk2_ref, o_ref, s_ref = rest[2 * n_pages:]
    n_rows = q_ref.shape[1]
    half = n_rows // 2
    page_rows = k_pages[0].shape[0]
    q = q_ref[0].astype(BF16)

    row_h = lax.broadcasted_iota(jnp.int32, (n_rows, page_rows), 0) % N_DIFF_HEADS
    col_h = lax.broadcasted_iota(jnp.int32, (n_rows, page_rows), 1) % N_DIFF_HEADS
    same_head = row_h == col_h
    m = jnp.full((n_rows, 1), NEG_BIG, F32)
    for p in range(n_pages):
        s = jnp.where(same_head, _dot_nt(q, k_pages[p][...].astype(BF16)), NEG_BIG)
        s_ref[:, p * page_rows:(p + 1) * page_rows] = s
        m = jnp.maximum(m, jnp.max(s, axis=-1, keepdims=True))

    n_new = kn_ref.shape[1]
    r = lax.broadcasted_iota(jnp.int32, (n_rows, n_new), 0)
    c = lax.broadcasted_iota(jnp.int32, (n_rows, n_new), 1)
    ok = (r % N_DIFF_HEADS == c % N_DIFF_HEADS) & (c // N_DIFF_HEADS <= (r % half) // N_DIFF_HEADS)
    s_new = jnp.where(ok, _dot_nt(q, kn_ref[0].astype(BF16)), NEG_BIG)
    m = jnp.maximum(m, jnp.max(s_new, axis=-1, keepdims=True))

    e_new = jnp.exp(s_new - m)
    l = jnp.sum(e_new, axis=-1, keepdims=True)
    for p in range(n_pages):
        cols = slice(p * page_rows, (p + 1) * page_rows)
        e = jnp.exp(s_ref[:, cols] - m)
        s_ref[:, cols] = e
        l = l + jnp.sum(e, axis=-1, keepdims=True)

    lam = _lambda(lq1_ref, lk1_ref, lq2_ref, lk2_ref, lam_init)
    inv = 1.0 / l
    w1 = inv[:half]
    w2 = lam * inv[half:]
    a_new = (e_new[:half] * w1 - e_new[half:] * w2).astype(BF16)
    o = _dot(a_new, vn_ref[0].astype(BF16))
    for p in range(n_pages):
        cols = slice(p * page_rows, (p + 1) * page_rows)
        a = (s_ref[:half, cols] * w1 - s_ref[half:, cols] * w2).astype(BF16)
        o = o + _dot(a, v_pages[p][...].astype(BF16))
    o_ref[0] = _rms(o, g_ref[...]) * (1.0 - lam_init)


def _diff_sample(qbig, k_new, v_new, k_cache, v_cache, page_table, layer, lam_params, g_rows, lam_init):
    db, n_rows, _ = qbig.shape
    n_pages = page_table.shape[1]
    page_rows = k_cache.shape[2]
    n_new = k_new.shape[1]

    def page_spec(p):
        return pl.BlockSpec((None, None, page_rows, LANES), lambda b, pt: (layer, pt[b * n_pages + p], 0, 0))

    small = lambda rows: pl.BlockSpec((1, rows, LANES), lambda b, pt: (b, 0, 0))
    lam_spec = pl.BlockSpec((1, HEAD_DIM), lambda b, pt: (0, 0))
    grid_spec = pltpu.PrefetchScalarGridSpec(
        num_scalar_prefetch=1,
        grid=(db,),
        in_specs=[small(n_rows), small(n_new), small(n_new)]
        + [page_spec(p) for p in range(n_pages)] * 2
        + [pl.BlockSpec((n_rows // 2, LANES), lambda b, pt: (0, 0)), lam_spec, lam_spec, lam_spec, lam_spec],
        out_specs=small(n_rows // 2),
        scratch_shapes=[pltpu.VMEM((n_rows, n_pages * page_rows), F32)],
    )
    return pl.pallas_call(
        functools.partial(_diff_sample_kernel, n_pages=n_pages, lam_init=lam_init),
        grid_spec=grid_spec,
        out_shape=jax.ShapeDtypeStruct((db, n_rows // 2, LANES), F32),
        compiler_params=_cparams(("parallel",)),
        name="diff_attn_sample",
    )(page_table.reshape(-1), qbig, k_new, v_new, *([k_cache] * n_pages), *([v_cache] * n_pages), g_rows,
      *lam_params)


def _merge_kernel(x_ref, rg_ref, ro_ref, do_ref, w_ref, g_ref, o_ref):
    a = jnp.concatenate([_silu(rg_ref[...]) * ro_ref[...], do_ref[...]], axis=-1).astype(BF16)
    o_ref[...] = x_ref[...] + _rms(_dot(a, w_ref[...]), g_ref[...])


def _merge(x, z, ro, do, w_bf, g, *, tm):
    m, d = x.shape
    rg_col = 3
    row = lambda w: pl.BlockSpec((tm, w), lambda i: (i, 0))
    return pl.pallas_call(
        _merge_kernel,
        grid=(m // tm,),
        in_specs=[row(d), pl.BlockSpec((tm, SECTION), lambda i: (i, rg_col)), row(SECTION), row(SECTION),
                  pl.BlockSpec(w_bf.shape, lambda i: (0, 0)), pl.BlockSpec((1, d), lambda i: (0, 0))],
        out_specs=row(d),
        out_shape=jax.ShapeDtypeStruct((m, d), F32),
        compiler_params=_cparams(("parallel",)),
        name="mixer_merge",
    )(x, z, ro, do, w_bf, g.reshape(1, d))


def _mem_prompt_kernel(x_ref, gpre_ref, wq_ref, mk_ref, mv_ref, wo_ref, gpost_ref, o_ref):
    x = x_ref[...]
    q = _dot(_rms(x, gpre_ref[...]).astype(BF16), wq_ref[...])
    hd = q.shape[1] // N_MEM_HEADS
    outs = []
    for h in range(N_MEM_HEADS):
        cols = slice(h * hd, (h + 1) * hd)
        s = _dot_nt(q[:, cols].astype(BF16), mk_ref[0, :, cols].astype(BF16)) * (hd ** -0.5)
        e = jnp.exp(s - jnp.max(s, axis=-1, keepdims=True))
        p = e / jnp.sum(e, axis=-1, keepdims=True)
        outs.append(_dot(p.astype(BF16), mv_ref[0, :, cols].astype(BF16)))
    o = jnp.concatenate(outs, axis=-1).astype(BF16)
    o_ref[...] = x + _rms(_dot(o, wo_ref[...]), gpost_ref[...])


def _mem_prompt(x, g_pre, wq_bf, mk, mv, wo_bf, g_post, seq, *, tm):
    m, d = x.shape
    n_mem = mk.shape[1]
    blocks_per_seq = seq // tm
    vec = pl.BlockSpec((1, d), lambda i: (0, 0))
    mat = pl.BlockSpec((d, d), lambda i: (0, 0))
    mem = pl.BlockSpec((1, n_mem, d), lambda i: (i // blocks_per_seq, 0, 0))
    row = pl.BlockSpec((tm, d), lambda i: (i, 0))
    return pl.pallas_call(
        _mem_prompt_kernel,
        grid=(m // tm,),
        in_specs=[row, vec, mat, mem, mem, mat, vec],
        out_specs=row,
        out_shape=jax.ShapeDtypeStruct((m, d), F32),
        compiler_params=_cparams(("parallel",)),
        name="mem_attn_prompt",
    )(x, g_pre.reshape(1, d), wq_bf, mk, mv, wo_bf, g_post.reshape(1, d))


def _mem_sample_kernel(q_ref, k_ref, v_ref, o_ref):
    bb, n_rows, hd = q_ref.shape
    key_rows = k_ref.shape[1]
    row_h = lax.broadcasted_iota(jnp.int32, (n_rows, key_rows), 0) % N_MEM_HEADS
    col_h = lax.broadcasted_iota(jnp.int32, (n_rows, key_rows), 1) % N_MEM_HEADS
    same_head = row_h == col_h
    for b in range(bb):
        s = _dot_nt(q_ref[b].astype(BF16), k_ref[b].astype(BF16)) * (hd ** -0.5)
        s = jnp.where(same_head, s, NEG_BIG)
        e = jnp.exp(s - jnp.max(s, axis=-1, keepdims=True))
        p = e / jnp.sum(e, axis=-1, keepdims=True)
        o_ref[b] = _dot(p.astype(BF16), v_ref[b].astype(BF16))


def _mem_sample(q_rows, k_cache, v_cache, layer, *, bb):
    db, n_rows, hd = q_rows.shape
    key_rows = k_cache.shape[2]
    qspec = pl.BlockSpec((bb, n_rows, hd), lambda i: (i, 0, 0))
    cspec = pl.BlockSpec((None, bb, key_rows, hd), lambda i: (layer, i, 0, 0))
    return pl.pallas_call(
        _mem_sample_kernel,
        grid=(db // bb,),
        in_specs=[qspec, cspec, cspec],
        out_specs=qspec,
        out_shape=jax.ShapeDtypeStruct(q_rows.shape, F32),
        compiler_params=_cparams(("parallel",)),
        name="mem_attn_sample",
    )(q_rows, k_cache, v_cache)


def _out_proj_kernel(x_ref, a_ref, w_ref, g_ref, o_ref):
    o_ref[...] = x_ref[...] + _rms(_dot(a_ref[...].astype(BF16), w_ref[...]), g_ref[...])


def _out_proj(x, a, w_bf, g, *, tm):
    m, d = x.shape
    row = pl.BlockSpec((tm, d), lambda i: (i, 0))
    return pl.pallas_call(
        _out_proj_kernel,
        grid=(m // tm,),
        in_specs=[row, pl.BlockSpec((tm, a.shape[1]), lambda i: (i, 0)), pl.BlockSpec(w_bf.shape, lambda i: (0, 0)),
                  pl.BlockSpec((1, d), lambda i: (0, 0))],
        out_specs=row,
        out_shape=jax.ShapeDtypeStruct((m, d), F32),
        compiler_params=_cparams(("parallel",)),
        name="out_proj",
    )(x, a, w_bf, g.reshape(1, d))


def _ffn_kernel(x_ref, gpre_ref, wug_ref, wuv_ref, cwg_ref, cwv_ref, cbg_ref, cbv_ref, wd_ref, hg_ref, hv_ref,
                gpost_ref, o_ref, tg_ref, tv_ref, xn_ref, acc_ref, halo_g_ref, halo_v_ref, *, shift,
                blocks_per_seq):
    i = pl.program_id(0)
    j = pl.program_id(1)
    tm = x_ref.shape[0]

    @pl.when(j == 0)
    def _():
        xn_ref[...] = _rms(x_ref[...], gpre_ref[...]).astype(BF16)
        acc_ref[...] = jnp.zeros_like(acc_ref)

    seq_start = (i % blocks_per_seq) == 0

    def conv(w_ref, cw_ref, cb_ref, hist_ref, halo_ref, tail_ref):
        u = _dot(xn_ref[...], w_ref[...])
        prev = jnp.where(seq_start, hist_ref[0], halo_ref[j])
        if shift == 1:
            rows = lax.broadcasted_iota(jnp.int32, u.shape, 0)
            u1 = jnp.where(rows == 0, prev[1:2], pltpu.roll(u, 1, 0))
            u2 = jnp.where(rows == 0, prev[0:1], jnp.where(rows == 1, prev[1:2], pltpu.roll(u, 2, 0)))
        else:
            ue = jnp.concatenate([prev, u], axis=0)
            u2 = ue[:tm]
            u1 = ue[shift:shift + tm]
        tail = u[tm - 2 * shift:]
        halo_ref[j] = tail
        tail_ref[0] = tail
        cw = cw_ref[...]
        return cb_ref[...] + cw[0:1] * u2 + cw[1:2] * u1 + cw[2:3] * u

    gate = conv(wug_ref, cwg_ref, cbg_ref, hg_ref, halo_g_ref, tg_ref)
    val = conv(wuv_ref, cwv_ref, cbv_ref, hv_ref, halo_v_ref, tv_ref)
    acc_ref[...] += _dot((_silu(gate) * val).astype(BF16), wd_ref[...])

    @pl.when(j == pl.num_programs(1) - 1)
    def _():
        o_ref[...] = x_ref[...] + _rms(acc_ref[...], gpost_ref[...])


def _conv_ffn(x, g_pre, wu_bf, conv_w, conv_b, wd_bf, hist, g_post, *, tm, tf, shift, blocks_per_seq):
    m, d = x.shape
    f = wd_bf.shape[0]
    nj = f // tf
    nblk = m // tm
    halo = 2 * shift
    up = lambda off: pl.BlockSpec((d, tf), lambda i, j: (0, off + j))
    cw = lambda off: pl.BlockSpec((CONV_WIDTH, tf), lambda i, j: (0, off + j))
    cb = lambda off: pl.BlockSpec((1, tf), lambda i, j: (0, off + j))
    hs = lambda off: pl.BlockSpec((1, halo, tf), lambda i, j: (i // blocks_per_seq, 0, off + j))
    vec = pl.BlockSpec((1, d), lambda i, j: (0, 0))
    row = pl.BlockSpec((tm, d), lambda i, j: (i, 0))
    tail = pl.BlockSpec((1, halo, tf), lambda i, j: (i, 0, j))
    return pl.pallas_call(
        functools.partial(_ffn_kernel, shift=shift, blocks_per_seq=blocks_per_seq),
        grid=(nblk, nj),
        in_specs=[row, vec, up(0), up(nj), cw(0), cw(nj), cb(0), cb(nj),
                  pl.BlockSpec((tf, d), lambda i, j: (j, 0)), hs(0), hs(nj), vec],
        out_specs=[row, tail, tail],
        out_shape=[jax.ShapeDtypeStruct((m, d), F32),
                   jax.ShapeDtypeStruct((nblk, halo, f), F32),
                   jax.ShapeDtypeStruct((nblk, halo, f), F32)],
        scratch_shapes=[pltpu.VMEM((tm, d), BF16), pltpu.VMEM((tm, d), F32),
                        pltpu.VMEM((nj, halo, tf), F32), pltpu.VMEM((nj, halo, tf), F32)],
        compiler_params=_cparams(("arbitrary", "arbitrary")),
        name="conv_ffn",
    )(x, g_pre.reshape(1, d), wu_bf, wu_bf, conv_w, conv_w, conv_b.reshape(1, -1), conv_b.reshape(1, -1), wd_bf,
      hist, hist, g_post.reshape(1, d))


def _rope_tables(pos):
    inv = 1.0 / (ROPE_THETA ** (jnp.arange(0, HEAD_DIM, 2, dtype=F32) / HEAD_DIM))
    ang = pos.astype(F32)[:, None] * inv[None, :]
    ang = jnp.concatenate([ang, ang], axis=-1)
    cos, sin = jnp.cos(ang), jnp.sin(ang)
    first_half = jnp.arange(HEAD_DIM) < HEAD_DIM // 2
    sin_a = jnp.where(first_half[None], -sin, 0.0)
    sin_b = jnp.where(first_half[None], 0.0, sin)
    rep = LANES // HEAD_DIM
    return tuple(jnp.tile(t, (1, rep)) for t in (cos, sin_a, sin_b))


def _tile_for(n, target):
    t = min(n, target)
    assert n % t == 0, (n, t)
    return t


def kernel(x_prompt, x_sample, mem_prompt, state_ret, cache_diff_k, cache_diff_v, page_table, cache_mem_k, cache_mem_v, state_conv, g_mix_pre, g_mix_post, w_in, ret_norm_g, lam_q1, lam_k1, lam_q2, lam_k2, diff_norm_g, w_out, g_mem_pre, g_mem_post, g_mem_kv, w_mq, w_mk, w_mv, w_mo, g_ffn_pre, g_ffn_post, w_up, conv_w, conv_b, w_down):
    B, S, D = x_prompt.shape
    DB, DS, _ = x_sample.shape
    depth = w_in.shape[0]
    n_pool, page = cache_diff_k.shape[1], cache_diff_k.shape[2]
    past = page_table.shape[1] * page
    n_mem = mem_prompt.shape[1]
    two_f = w_up.shape[2]
    mem_hd = D // N_MEM_HEADS
    mp, ms = B * S, DB * DS

    tm_p = _tile_for(S, 512)
    tile_attn = _tile_for(S, 512)
    tf = 256
    assert (two_f // 2) % tf == 0

    tabs_p = _rope_tables(jnp.arange(S, dtype=jnp.int32))
    tabs_s = _rope_tables(jnp.tile(past + jnp.arange(DS, dtype=jnp.int32), DB))
    no_tabs_mem = tuple(jnp.zeros((B * n_mem, LANES), F32) for _ in range(3))
    no_tabs_s = tuple(jnp.zeros((ms, LANES), F32) for _ in range(3))

    kc = cache_diff_k.reshape(depth, n_pool, page * N_DIFF_HEADS, DIFF_DV)
    vc = cache_diff_v.reshape(depth, n_pool, page * N_DIFF_HEADS, DIFF_DV)
    mkc = cache_mem_k.reshape(depth, DB, n_mem * N_MEM_HEADS, mem_hd)
    mvc = cache_mem_v.reshape(depth, DB, n_mem * N_MEM_HEADS, mem_hd)

    xp = x_prompt.reshape(mp, D)
    xs = x_sample.reshape(ms, D)
    mem2d = mem_prompt.reshape(B * n_mem, D)
    zero_hist = jnp.zeros((B, 2, two_f), F32)
    ret_pad = 8

    outs = {k: [] for k in ("p_ret", "p_dk", "p_dv", "p_mk", "p_mv", "p_conv", "s_ret", "s_dk", "s_dv", "s_conv")}
    rope_secs = (0, 1, 4, 5)
    for l in range(depth):
        lam_init = 0.8 - 0.6 * math.exp(-0.3 * l)
        lam_params = [t[l].reshape(1, HEAD_DIM) for t in (lam_q1, lam_k1, lam_q2, lam_k2)]
        w_in_bf = w_in[l].astype(BF16)
        w_out_bf = w_out[l].astype(BF16)
        w_mq_bf = w_mq[l].astype(BF16)
        w_mo_bf = w_mo[l].astype(BF16)
        w_mkv_bf = jnp.concatenate([w_mk[l], w_mv[l]], axis=1).astype(BF16)
        w_up_bf = w_up[l].astype(BF16)
        w_down_bf = w_down[l].astype(BF16)

        z, dk, dv = _project(xp, g_mix_pre[l], w_in_bf, tabs_p, tm=tm_p, tn=SECTION, rope_sections=rope_secs,
                             k_section=1, n_main=5)
        ro, s_fin = _retention_prompt(z, ret_norm_g[l], B, S, rows_per_step=tm_p)
        do = _diff_prompt(z, dk, dv, lam_params, diff_norm_g[l], lam_init, B, S, tile=tile_attn)
        xp = _merge(xp, z, ro, do, w_out_bf, g_mix_post[l], tm=tm_p)
        outs["p_ret"].append(s_fin)
        outs["p_dk"].append(dk.reshape(B, S, N_DIFF_HEADS, DIFF_DV))
        outs["p_dv"].append(dv.reshape(B, S, N_DIFF_HEADS, DIFF_DV))

        zs, dks, dvs = _project(xs, g_mix_pre[l], w_in_bf, tabs_s, tm=ms, tn=SECTION, rope_sections=rope_secs,
                                k_section=1, n_main=5)

        def heads_first(t):
            t = t.reshape(DB, DS, N_RET_HEADS, HEAD_DIM).transpose(0, 2, 1, 3)
            return jnp.pad(t, ((0, 0), (0, 0), (0, ret_pad - DS), (0, 0)))

        ro_s, s_new = _retention_sample(heads_first(zs[:, :SECTION]), heads_first(zs[:, SECTION:2 * SECTION]),
                                        heads_first(zs[:, 2 * SECTION:3 * SECTION]), state_ret[l], ret_norm_g[l],
                                        DS, bb=8)
        ro_s = ro_s[:, :, :DS].transpose(0, 2, 1, 3).reshape(ms, SECTION)

        dq = zs[:, 4 * SECTION:].reshape(DB, DS, N_DIFF_HEADS, 2, HEAD_DIM) * (HEAD_DIM ** -0.5)
        dq = dq.transpose(0, 3, 1, 2, 4)
        zeros = jnp.zeros_like(dq[:, 0])
        qbig = jnp.stack([jnp.concatenate([dq[:, 0], zeros], -1), jnp.concatenate([zeros, dq[:, 1]], -1)], axis=1)
        qbig = qbig.reshape(DB, 2 * DS * N_DIFF_HEADS, DIFF_DV)
        g_rows = jnp.tile(diff_norm_g[l], (DS, 1))
        do_s = _diff_sample(qbig, dks.reshape(DB, DS * N_DIFF_HEADS, DIFF_DV),
                            dvs.reshape(DB, DS * N_DIFF_HEADS, DIFF_DV), kc, vc, page_table, l, lam_params, g_rows,
                            lam_init)
        xs = _merge(xs, zs, ro_s, do_s.reshape(ms, N_DIFF_HEADS * DIFF_DV), w_out_bf, g_mix_post[l], tm=ms)
        outs["s_ret"].append(s_new)
        outs["s_dk"].append(dks.reshape(DB, DS, N_DIFF_HEADS, DIFF_DV))
        outs["s_dv"].append(dvs.reshape(DB, DS, N_DIFF_HEADS, DIFF_DV))

        mkv, = _project(mem2d, g_mem_kv[l], w_mkv_bf, no_tabs_mem, tm=B * n_mem, tn=SECTION)
        mk = mkv[:, :D].reshape(B, n_mem, D)
        mv = mkv[:, D:].reshape(B, n_mem, D)
        xp = _mem_prompt(xp, g_mem_pre[l], w_mq_bf, mk, mv, w_mo_bf, g_mem_post[l], S, tm=tm_p)
        qs, = _project(xs, g_mem_pre[l], w_mq_bf, no_tabs_s, tm=ms, tn=SECTION)
        om = _mem_sample(qs.reshape(DB, DS * N_MEM_HEADS, mem_hd), mkc, mvc, l, bb=4)
        xs = _out_proj(xs, om.reshape(ms, D), w_mo_bf, g_mem_post[l], tm=ms)
        outs["p_mk"].append(mk.reshape(B, n_mem, N_MEM_HEADS, mem_hd))
        outs["p_mv"].append(mv.reshape(B, n_mem, N_MEM_HEADS, mem_hd))

        xp, tg, tv = _conv_ffn(xp, g_ffn_pre[l], w_up_bf, conv_w[l], conv_b[l], w_down_bf, zero_hist, g_ffn_post[l],
                               tm=tm_p, tf=tf, shift=1, blocks_per_seq=S // tm_p)
        last = jnp.arange(B) * (S // tm_p) + (S // tm_p - 1)
        outs["p_conv"].append(jnp.concatenate([tg[last], tv[last]], axis=-1))
        xs_t = xs.reshape(DB, DS, D).transpose(1, 0, 2).reshape(ms, D)
        hist_s = state_conv[l].transpose(1, 0, 2).reshape(1, 2 * DB, two_f)
        xs_t, tg, tv = _conv_ffn(xs_t, g_ffn_pre[l], w_up_bf, conv_w[l], conv_b[l], w_down_bf, hist_s, g_ffn_post[l],
                                 tm=ms, tf=tf, shift=DB, blocks_per_seq=1)
        xs = xs_t.reshape(DS, DB, D).transpose(1, 0, 2).reshape(ms, D)
        outs["s_conv"].append(jnp.concatenate([tg[0], tv[0]], axis=-1).reshape(2, DB, two_f).transpose(1, 0, 2))

    st = lambda k: jnp.stack(outs[k])
    return (xp.reshape(B, S, D), xs.reshape(DB, DS, D), st("p_ret"), st("p_dk"), st("p_dv"), st("p_mk"), st("p_mv"),
            st("p_conv"), st("s_ret"), st("s_dk"), st("s_dv"), st("s_conv"))
```

```python
import functools
import math

import jax
import jax.numpy as jnp
from jax import lax
from jax.experimental import pallas as pl
from jax.experimental.pallas import tpu as pltpu

F32 = jnp.float32
BF16 = jnp.bfloat16

HEAD_DIM = 64
N_RET_HEADS = 8
N_DIFF_HEADS = 4
DIFF_DV = 2 * HEAD_DIM
SECTION = 512
N_SECTIONS = 7
N_MEM_HEADS = 4
CONV_WIDTH = 3
ROPE_THETA = 10000.0
EPS = 1e-6
NEG_BIG = -1e30
LOG2_E = math.log2(math.e)
RET_CHUNK = 128

LANES = 128
BF16_SUBLANES = 16
ONES_ROWS = BF16_SUBLANES
V7X_VMEM_BYTES = 64 * 1024 * 1024
VMEM_LIMIT = (V7X_VMEM_BYTES * 3) // 4


def _cparams(semantics):
    return pltpu.CompilerParams(dimension_semantics=semantics, vmem_limit_bytes=VMEM_LIMIT)


def _rms(x, g):
    return x * lax.rsqrt(jnp.mean(x * x, axis=-1, keepdims=True) + EPS) * g


def _dot(a, b):
    return jnp.dot(a, b, preferred_element_type=F32)


def _dot_nt(a, b):
    return lax.dot_general(a, b, (((1,), (1,)), ((), ())), preferred_element_type=F32)


def _silu(x):
    return x * (1.0 / (1.0 + jnp.exp(-x)))


def _proj_kernel(x_ref, g_ref, w_ref, cos_ref, sa_ref, sb_ref, *out_and_scratch, rope_sections, k_section,
                 n_main):
    outs, xn_ref = out_and_scratch[:-1], out_and_scratch[-1]
    j = pl.program_id(1)

    @pl.when(j == 0)
    def _():
        xn_ref[...] = _rms(x_ref[...], g_ref[...]).astype(BF16)

    acc = _dot(xn_ref[...], w_ref[...])
    n_sections = n_main + len(outs) - 1
    for sec in range(n_sections):
        o_ref = outs[0] if sec < n_main else outs[sec - n_main + 1]

        @pl.when(j == sec)
        def _(o_ref=o_ref, sec=sec):
            if sec in rope_sections:
                scale = HEAD_DIM ** -0.5 if sec == k_section else 1.0
                c, sa, sb = cos_ref[...], sa_ref[...], sb_ref[...]
                for t in range(acc.shape[1] // LANES):
                    a = acc[:, t * LANES:(t + 1) * LANES]
                    r = a * c + pltpu.roll(a, LANES - HEAD_DIM // 2, 1) * sa + pltpu.roll(a, HEAD_DIM // 2, 1) * sb
                    o_ref[:, t * LANES:(t + 1) * LANES] = r * scale if scale != 1.0 else r
            else:
                o_ref[...] = acc


def _project(x, g, w_bf, tabs, *, tm, tn, rope_sections=(), k_section=-1, n_main=None):
    m, d = x.shape
    n = w_bf.shape[1]
    n_sec = n // tn
    n_main = n_sec if n_main is None else n_main
    cos, sa, sb = tabs
    tab_blocks = cos.shape[0] // tm
    tab_spec = pl.BlockSpec((tm, LANES), lambda i, j: (i % tab_blocks, 0))
    out_shape = [jax.ShapeDtypeStruct((m, n_main * tn), F32)]
    out_specs = [pl.BlockSpec((tm, tn), lambda i, j: (i, jnp.minimum(j, n_main - 1)))]
    for _ in range(n_sec - n_main):
        out_shape.append(jax.ShapeDtypeStruct((m, tn), F32))
        out_specs.append(pl.BlockSpec((tm, tn), lambda i, j: (i, 0)))
    kern = functools.partial(_proj_kernel, rope_sections=tuple(rope_sections), k_section=k_section, n_main=n_main)
    return pl.pallas_call(
        kern,
        grid=(m // tm, n_sec),
        in_specs=[pl.BlockSpec((tm, d), lambda i, j: (i, 0)),
                  pl.BlockSpec((1, d), lambda i, j: (0, 0)),
                  pl.BlockSpec((d, tn), lambda i, j: (0, j)),
                  tab_spec, tab_spec, tab_spec],
        out_specs=out_specs,
        out_shape=out_shape,
        scratch_shapes=[pltpu.VMEM((tm, d), BF16)],
        compiler_params=_cparams(("parallel", "arbitrary")),
        name="project",
    )(x, g.reshape(1, d), w_bf, cos, sa, sb)


def _ret_prompt_kernel(q_ref, k_ref, v_ref, dec_ref, rw_ref, ww_ref, cd_ref, g_ref, o_ref, sfin_ref, s_ref, *,
                       n_chunks):
    r = pl.program_id(1)

    @pl.when(r == 0)
    def _():
        s_ref[...] = jnp.zeros_like(s_ref)

    def chunk(c, carry):
        row = pl.multiple_of(c * RET_CHUNK, RET_CHUNK)
        rows = pl.ds(row, RET_CHUNK)
        for h in range(N_RET_HEADS):
            cols = slice(h * HEAD_DIM, (h + 1) * HEAD_DIM)
            q = q_ref[rows, cols]
            k = k_ref[rows, cols]
            vb = v_ref[rows, cols].astype(BF16)
            qb = q.astype(BF16)
            s = s_ref[h]
            scores = _dot_nt(qb, k.astype(BF16)) * dec_ref[h]
            o = _dot(scores.astype(BF16), vb) + _dot(qb, s.astype(BF16)) * rw_ref[h]
            kw_t = (k * ww_ref[h]).T.astype(BF16)
            s_ref[h] = s * cd_ref[h] + _dot(kw_t, vb)
            mu = jnp.mean(o, axis=-1, keepdims=True)
            var = jnp.mean(jnp.square(o - mu), axis=-1, keepdims=True)
            o_ref[rows, cols] = (o - mu) * lax.rsqrt(var + EPS) * g_ref[h]
        return carry

    lax.fori_loop(0, n_chunks, chunk, 0)

    @pl.when(r == pl.num_programs(1) - 1)
    def _():
        sfin_ref[0] = s_ref[...]


def _retention_tables(length):
    log_g = jnp.log1p(-jnp.exp2(-5.0 - jnp.arange(N_RET_HEADS, dtype=F32)))
    n = jnp.arange(length, dtype=F32)
    diff = n[:, None] - n[None, :]
    dec = jnp.where((diff >= 0)[None], jnp.exp(log_g[:, None, None] * jnp.maximum(diff, 0.0)[None]), 0.0)
    read_w = jnp.exp(log_g[:, None] * (n + 1.0)[None])
    write_w = jnp.exp(log_g[:, None] * (length - 1.0 - n)[None])
    chunk_decay = jnp.exp(log_g * length)
    return dec, read_w, write_w, chunk_decay


def _retention_prompt(z, ret_g, batch, seq, *, rows_per_step):
    m = z.shape[0]
    nr = seq // rows_per_step
    dec, read_w, write_w, chunk_decay = _retention_tables(RET_CHUNK)
    rw = jnp.broadcast_to(read_w[:, :, None], (N_RET_HEADS, RET_CHUNK, HEAD_DIM))
    ww = jnp.broadcast_to(write_w[:, :, None], (N_RET_HEADS, RET_CHUNK, HEAD_DIM))
    cd = jnp.broadcast_to(chunk_decay[:, None, None], (N_RET_HEADS, HEAD_DIM, HEAD_DIM))
    full3 = lambda shape: pl.BlockSpec(shape, lambda b, r: (0, 0, 0))
    sec = lambda c: pl.BlockSpec((rows_per_step, SECTION), lambda b, r: (b * nr + r, c))
    return pl.pallas_call(
        functools.partial(_ret_prompt_kernel, n_chunks=rows_per_step // RET_CHUNK),
        grid=(batch, nr),
        in_specs=[sec(0), sec(1), sec(2),
                  full3((N_RET_HEADS, RET_CHUNK, RET_CHUNK)),
                  full3((N_RET_HEADS, RET_CHUNK, HEAD_DIM)),
                  full3((N_RET_HEADS, RET_CHUNK, HEAD_DIM)),
                  full3((N_RET_HEADS, HEAD_DIM, HEAD_DIM)),
                  full3((N_RET_HEADS, 1, HEAD_DIM))],
        out_specs=[pl.BlockSpec((rows_per_step, SECTION), lambda b, r: (b * nr + r, 0)),
                   pl.BlockSpec((1, N_RET_HEADS, HEAD_DIM, HEAD_DIM), lambda b, r: (b, 0, 0, 0))],
        out_shape=[jax.ShapeDtypeStruct((m, SECTION), F32),
                   jax.ShapeDtypeStruct((batch, N_RET_HEADS, HEAD_DIM, HEAD_DIM), F32)],
        scratch_shapes=[pltpu.VMEM((N_RET_HEADS, HEAD_DIM, HEAD_DIM), F32)],
        compiler_params=_cparams(("parallel", "arbitrary")),
        name="retention_prompt",
    )(z, z, z, dec, rw, ww, cd, ret_g.reshape(N_RET_HEADS, 1, HEAD_DIM))


def _ret_sample_kernel(q_ref, k_ref, v_ref, s_ref, dec_ref, rw_ref, ww_ref, cd_ref, g_ref, o_ref, snew_ref):
    n = q_ref.shape[0] * q_ref.shape[1]
    length = q_ref.shape[2]
    q = q_ref[...].reshape(n, length, HEAD_DIM)
    k = k_ref[...].reshape(n, length, HEAD_DIM)
    vb = v_ref[...].reshape(n, length, HEAD_DIM).astype(BF16)
    s = s_ref[...].reshape(n, HEAD_DIM, HEAD_DIM)
    qb = q.astype(BF16)
    scores = jnp.einsum('nqd,nkd->nqk', qb, k.astype(BF16), preferred_element_type=F32) * dec_ref[...]
    o = jnp.einsum('nqk,nke->nqe', scores.astype(BF16), vb, preferred_element_type=F32)
    o = o + jnp.einsum('nqd,nde->nqe', qb, s.astype(BF16), preferred_element_type=F32) * rw_ref[...]
    kw = (k * ww_ref[...]).astype(BF16)
    s_new = s * cd_ref[...] + jnp.einsum('nld,nle->nde', kw, vb, preferred_element_type=F32)
    mu = jnp.mean(o, axis=-1, keepdims=True)
    var = jnp.mean(jnp.square(o - mu), axis=-1, keepdims=True)
    o_ref[...] = ((o - mu) * lax.rsqrt(var + EPS) * g_ref[...]).reshape(o_ref.shape)
    snew_ref[...] = s_new.reshape(snew_ref.shape)


def _retention_sample(q, k, v, state, ret_g, n_tokens, *, bb):
    db, nh, length, _ = q.shape
    dec, read_w, write_w, chunk_decay = _retention_tables(n_tokens)
    pad = length - n_tokens
    dec = jnp.pad(dec, ((0, 0), (0, pad), (0, pad)))
    read_w = jnp.pad(read_w, ((0, 0), (0, pad)))
    write_w = jnp.pad(write_w, ((0, 0), (0, pad)))
    n = bb * nh
    tile = lambda t: jnp.tile(t, (bb,) + (1,) * (t.ndim - 1))
    dec_t = tile(dec)
    rw_t = tile(jnp.broadcast_to(read_w[:, :, None], (nh, length, HEAD_DIM)))
    ww_t = tile(jnp.broadcast_to(write_w[:, :, None], (nh, length, HEAD_DIM)))
    cd_t = tile(jnp.broadcast_to(chunk_decay[:, None, None], (nh, HEAD_DIM, HEAD_DIM)))
    g_t = tile(jnp.broadcast_to(ret_g[:, None, :], (nh, length, HEAD_DIM)))
    blk = pl.BlockSpec((bb, nh, length, HEAD_DIM), lambda i: (i, 0, 0, 0))
    sblk = pl.BlockSpec((bb, nh, HEAD_DIM, HEAD_DIM), lambda i: (i, 0, 0, 0))
    full = lambda a: pl.BlockSpec(a.shape, lambda i: (0,) * a.ndim)
    return pl.pallas_call(
        _ret_sample_kernel,
        grid=(db // bb,),
        in_specs=[blk, blk, blk, sblk, full(dec_t), full(rw_t), full(ww_t), full(cd_t), full(g_t)],
        out_specs=[blk, sblk],
        out_shape=[jax.ShapeDtypeStruct(q.shape, F32), jax.ShapeDtypeStruct(state.shape, F32)],
        compiler_params=_cparams(("parallel",)),
        name="retention_sample",
    )(q, k, v, state, dec_t, rw_t, ww_t, cd_t, g_t)


def _lambda(lq1_ref, lk1_ref, lq2_ref, lk2_ref, lam_init):
    e1 = jnp.exp(jnp.sum(lq1_ref[...] * lk1_ref[...], axis=-1, keepdims=True))
    e2 = jnp.exp(jnp.sum(lq2_ref[...] * lk2_ref[...], axis=-1, keepdims=True))
    return e1 - e2 + lam_init


def _diff_prompt_kernel(qi_ref, ki_ref, q_ref, k_ref, v_ref, lq1_ref, lk1_ref, lq2_ref, lk2_ref, g_ref, o_ref,
                        m_ref, acc_ref, *, lam_init, heads):
    p = pl.program_id(2)
    qi = qi_ref[p]
    ki = ki_ref[p]
    tk = k_ref.shape[0]

    @pl.when(ki == 0)
    def _():
        m_ref[...] = jnp.full_like(m_ref, NEG_BIG)
        acc_ref[...] = jnp.zeros_like(acc_ref)

    def step(masked):
        ones = jnp.ones((ONES_ROWS, tk), BF16)

        def scores(idx):
            h, c = divmod(idx, 2)
            cols = slice(h * DIFF_DV + c * HEAD_DIM, h * DIFF_DV + (c + 1) * HEAD_DIM)
            q = (q_ref[:, cols] * (HEAD_DIM ** -0.5 * LOG2_E)).astype(BF16)
            return _dot_nt(k_ref[:, cols].astype(BF16), q)

        def accumulate(idx, s, v_ext):
            if masked:
                key = lax.broadcasted_iota(jnp.int32, s.shape, 0)
                qry = lax.broadcasted_iota(jnp.int32, s.shape, 1)
                s = jnp.where(key <= qry, s, NEG_BIG)
            m_old = m_ref[idx]
            m_new = jnp.maximum(m_old, jnp.max(s, axis=0, keepdims=True))
            alpha = jnp.exp2(m_old - m_new)
            e = jnp.exp2(s - m_new).astype(BF16)
            acc_ref[idx] = alpha * acc_ref[idx] + _dot(v_ext, e)
            m_ref[idx] = m_new

        n = 2 * heads
        s_next = scores(0)
        v_ext = None
        for idx in range(n):
            s_cur = s_next
            if idx + 1 < n:
                s_next = scores(idx + 1)
            if idx % 2 == 0:
                hcols = slice((idx // 2) * DIFF_DV, (idx // 2 + 1) * DIFF_DV)
                v_ext = jnp.concatenate([v_ref[:, hcols].T.astype(BF16), ones], axis=0)
            accumulate(idx, s_cur, v_ext)

    @pl.when(ki < qi)
    def _():
        step(False)

    @pl.when(ki == qi)
    def _():
        step(True)
        lam = _lambda(lq1_ref, lk1_ref, lq2_ref, lk2_ref, lam_init)
        for h in range(heads):
            a1 = acc_ref[2 * h]
            a2 = acc_ref[2 * h + 1]
            o_t = (a1[:DIFF_DV] / a1[DIFF_DV:DIFF_DV + 1] - lam * (a2[:DIFF_DV] / a2[DIFF_DV:DIFF_DV + 1]))
            o_ref[:, h * DIFF_DV:(h + 1) * DIFF_DV] = _rms(o_t.T, g_ref[h]) * (1.0 - lam_init)


def _diff_prompt(z, dk, dv, lam_params, diff_g, lam_init, batch, seq, *, tile, heads):
    m = z.shape[0]
    nt = seq // tile
    width = heads * DIFF_DV
    groups = N_DIFF_HEADS // heads
    pairs = [(qi, ki) for qi in range(nt) for ki in range(qi + 1)]
    qi_tab = jnp.asarray([p[0] for p in pairs], jnp.int32)
    ki_tab = jnp.asarray([p[1] for p in pairs], jnp.int32)
    q_col0 = 4 * SECTION // width
    lam_spec = pl.BlockSpec((1, HEAD_DIM), lambda b, h, p, qi, ki: (0, 0))
    grid_spec = pltpu.PrefetchScalarGridSpec(
        num_scalar_prefetch=2,
        grid=(batch, groups, len(pairs)),
        in_specs=[pl.BlockSpec((tile, width), lambda b, h, p, qi, ki: (b * nt + qi[p], q_col0 + h)),
                  pl.BlockSpec((tile, width), lambda b, h, p, qi, ki: (b * nt + ki[p], h)),
                  pl.BlockSpec((tile, width), lambda b, h, p, qi, ki: (b * nt + ki[p], h)),
                  lam_spec, lam_spec, lam_spec, lam_spec,
                  pl.BlockSpec((heads, 1, DIFF_DV), lambda b, h, p, qi, ki: (h, 0, 0))],
        out_specs=pl.BlockSpec((tile, width), lambda b, h, p, qi, ki: (b * nt + qi[p], h)),
        scratch_shapes=[pltpu.VMEM((2 * heads, 1, tile), F32),
                        pltpu.VMEM((2 * heads, DIFF_DV + ONES_ROWS, tile), F32)],
    )
    return pl.pallas_call(
        functools.partial(_diff_prompt_kernel, lam_init=lam_init, heads=heads),
        grid_spec=grid_spec,
        out_shape=jax.ShapeDtypeStruct((m, N_DIFF_HEADS * DIFF_DV), F32),
        compiler_params=_cparams(("parallel", "parallel", "arbitrary")),
        name="diff_attn_prompt",
    )(qi_tab, ki_tab, z, dk, dv, *lam_params, diff_g.reshape(N_DIFF_HEADS, 1, DIFF_DV))


def _diff_sample_kernel(pt_ref, q_ref, kn_ref, vn_ref, *rest, n_pages, lam_init):
    k_pages = rest[:n_pages]
    v_pages = rest[n_pages:2 * n_pages]
    g_ref, lq1_ref, lk1_ref, lq2_ref, lk2_ref, o_ref, s_ref = rest[2 * n_pages:]
    n_rows = q_ref.shape[1]
    half = n_rows // 2
    page_rows = k_pages[0].shape[0]
    q = q_ref[0].astype(BF16)

    row_h = lax.broadcasted_iota(jnp.int32, (n_rows, page_rows), 0) % N_DIFF_HEADS
    col_h = lax.broadcasted_iota(jnp.int32, (n_rows, page_rows), 1) % N_DIFF_HEADS
    same_head = row_h == col_h
    m = jnp.full((n_rows, 1), NEG_BIG, F32)
    for p in range(n_pages):
        s = jnp.where(same_head, _dot_nt(q, k_pages[p][...].astype(BF16)), NEG_BIG)
        s_ref[:, p * page_rows:(p + 1) * page_rows] = s
        m = jnp.maximum(m, jnp.max(s, axis=-1, keepdims=True))

    n_new = kn_ref.shape[1]
    r = lax.broadcasted_iota(jnp.int32, (n_rows, n_new), 0)
    c = lax.broadcasted_iota(jnp.int32, (n_rows, n_new), 1)
    ok = (r % N_DIFF_HEADS == c % N_DIFF_HEADS) & (c // N_DIFF_HEADS <= (r % half) // N_DIFF_HEADS)
    s_new = jnp.where(ok, _dot_nt(q, kn_ref[0].astype(BF16)), NEG_BIG)
    m = jnp.maximum(m, jnp.max(s_new, axis=-1, keepdims=True))

    e_new = jnp.exp(s_new - m)
    l = jnp.sum(e_new, axis=-1, keepdims=True)
    for p in range(n_pages):
        cols = slice(p * page_rows, (p + 1) * page_rows)
        e = jnp.exp(s_ref[:, cols] - m)
        s_ref[:, cols] = e
        l = l + jnp.sum(e, axis=-1, keepdims=True)

    lam = _lambda(lq1_ref, lk1_ref, lq2_ref, lk2_ref, lam_init)
    inv = 1.0 / l
    w1 = inv[:half]
    w2 = lam * inv[half:]
    a_new = (e_new[:half] * w1 - e_new[half:] * w2).astype(BF16)
    o = _dot(a_new, vn_ref[0].astype(BF16))
    for p in range(n_pages):
        cols = slice(p * page_rows, (p + 1) * page_rows)
        a = (s_ref[:half, cols] * w1 - s_ref[half:, cols] * w2).astype(BF16)
        o = o + _dot(a, v_pages[p][...].astype(BF16))
    o_ref[0] = _rms(o, g_ref[...]) * (1.0 - lam_init)


def _diff_sample(qbig, k_new, v_new, k_cache, v_cache, page_table, layer, lam_params, g_rows, lam_init):
    db, n_rows, _ = qbig.shape
    n_pages = page_table.shape[1]
    page_rows = k_cache.shape[2]
    n_new = k_new.shape[1]

    def page_spec(p):
        return pl.BlockSpec((None, None, page_rows, LANES), lambda b, pt: (layer, pt[b * n_pages + p], 0, 0))

    small = lambda rows: pl.BlockSpec((1, rows, LANES), lambda b, pt: (b, 0, 0))
    lam_spec = pl.BlockSpec((1, HEAD_DIM), lambda b, pt: (0, 0))
    grid_spec = pltpu.PrefetchScalarGridSpec(
        num_scalar_prefetch=1,
        grid=(db,),
        in_specs=[small(n_rows), small(n_new), small(n_new)]
        + [page_spec(p) for p in range(n_pages)] * 2
        + [pl.BlockSpec((n_rows // 2, LANES), lambda b, pt: (0, 0)), lam_spec, lam_spec, lam_spec, lam_spec],
        out_specs=small(n_rows // 2),
        scratch_shapes=[pltpu.VMEM((n_rows, n_pages * page_rows), F32)],
    )
    return pl.pallas_call(
        functools.partial(_diff_sample_kernel, n_pages=n_pages, lam_init=lam_init),
        grid_spec=grid_spec,
        out_shape=jax.ShapeDtypeStruct((db, n_rows // 2, LANES), F32),
        compiler_params=_cparams(("parallel",)),
        name="diff_attn_sample",
    )(page_table.reshape(-1), qbig, k_new, v_new, *([k_cache] * n_pages), *([v_cache] * n_pages), g_rows,
      *lam_params)


def _merge_kernel(x_ref, rg_ref, ro_ref, do_ref, w_ref, g_ref, o_ref):
    a = jnp.concatenate([_silu(rg_ref[...]) * ro_ref[...], do_ref[...]], axis=-1).astype(BF16)
    o_ref[...] = x_ref[...] + _rms(_dot(a, w_ref[...]), g_ref[...])


def _merge(x, z, ro, do, w_bf, g, *, tm):
    m, d = x.shape
    rg_col = 3
    row = lambda w: pl.BlockSpec((tm, w), lambda i: (i, 0))
    return pl.pallas_call(
        _merge_kernel,
        grid=(m // tm,),
        in_specs=[row(d), pl.BlockSpec((tm, SECTION), lambda i: (i, rg_col)), row(SECTION), row(SECTION),
                  pl.BlockSpec(w_bf.shape, lambda i: (0, 0)), pl.BlockSpec((1, d), lambda i: (0, 0))],
        out_specs=row(d),
        out_shape=jax.ShapeDtypeStruct((m, d), F32),
        compiler_params=_cparams(("parallel",)),
        name="mixer_merge",
    )(x, z, ro, do, w_bf, g.reshape(1, d))


def _mem_prompt_kernel(x_ref, gpre_ref, wq_ref, mk_ref, mv_ref, wo_ref, gpost_ref, o_ref):
    x = x_ref[...]
    q = _dot(_rms(x, gpre_ref[...]).astype(BF16), wq_ref[...])
    hd = q.shape[1] // N_MEM_HEADS
    outs = []
    for h in range(N_MEM_HEADS):
        cols = slice(h * hd, (h + 1) * hd)
        s = _dot_nt(q[:, cols].astype(BF16), mk_ref[0, :, cols].astype(BF16)) * (hd ** -0.5)
        e = jnp.exp(s - jnp.max(s, axis=-1, keepdims=True))
        p = e / jnp.sum(e, axis=-1, keepdims=True)
        outs.append(_dot(p.astype(BF16), mv_ref[0, :, cols].astype(BF16)))
    o = jnp.concatenate(outs, axis=-1).astype(BF16)
    o_ref[...] = x + _rms(_dot(o, wo_ref[...]), gpost_ref[...])


def _mem_prompt(x, g_pre, wq_bf, mk, mv, wo_bf, g_post, seq, *, tm):
    m, d = x.shape
    n_mem = mk.shape[1]
    blocks_per_seq = seq // tm
    vec = pl.BlockSpec((1, d), lambda i: (0, 0))
    mat = pl.BlockSpec((d, d), lambda i: (0, 0))
    mem = pl.BlockSpec((1, n_mem, d), lambda i: (i // blocks_per_seq, 0, 0))
    row = pl.BlockSpec((tm, d), lambda i: (i, 0))
    return pl.pallas_call(
        _mem_prompt_kernel,
        grid=(m // tm,),
        in_specs=[row, vec, mat, mem, mem, mat, vec],
        out_specs=row,
        out_shape=jax.ShapeDtypeStruct((m, d), F32),
        compiler_params=_cparams(("parallel",)),
        name="mem_attn_prompt",
    )(x, g_pre.reshape(1, d), wq_bf, mk, mv, wo_bf, g_post.reshape(1, d))


def _mem_sample_kernel(q_ref, k_ref, v_ref, o_ref):
    bb, n_rows, _ = q_ref.shape
    half = n_rows // 2
    key_rows = k_ref.shape[1]
    group = 2 * N_MEM_HEADS
    row = lax.broadcasted_iota(jnp.int32, (half, key_rows), 0)
    col = lax.broadcasted_iota(jnp.int32, (half, key_rows), 1)
    own = (col % group) == (row % N_MEM_HEADS)
    scale = (2 * LANES) ** -0.5
    for b in range(bb):
        sp = _dot_nt(q_ref[b].astype(BF16), k_ref[b].astype(BF16))
        s = (sp[:half] + pltpu.roll(sp[half:], key_rows - N_MEM_HEADS, 1)) * scale
        s = jnp.where(own, s, NEG_BIG)
        e = jnp.exp(s - jnp.max(s, axis=-1, keepdims=True))
        p = e / jnp.sum(e, axis=-1, keepdims=True)
        p2 = jnp.concatenate([p, pltpu.roll(p, N_MEM_HEADS, 1)], axis=0).astype(BF16)
        o = _dot(p2, v_ref[b].astype(BF16))
        o_ref[b] = jnp.concatenate([o[:half], o[half:]], axis=-1)


def _mem_sample(q_rows, k_cache, v_cache, layer, *, bb):
    db, n_rows, _ = q_rows.shape
    key_rows = k_cache.shape[2]
    qspec = pl.BlockSpec((bb, n_rows, LANES), lambda i: (i, 0, 0))
    cspec = pl.BlockSpec((None, bb, key_rows, LANES), lambda i: (layer, i, 0, 0))
    return pl.pallas_call(
        _mem_sample_kernel,
        grid=(db // bb,),
        in_specs=[qspec, cspec, cspec],
        out_specs=pl.BlockSpec((bb, n_rows // 2, 2 * LANES), lambda i: (i, 0, 0)),
        out_shape=jax.ShapeDtypeStruct((db, n_rows // 2, 2 * LANES), F32),
        compiler_params=_cparams(("parallel",)),
        name="mem_attn_sample",
    )(q_rows, k_cache, v_cache)


def _out_proj_kernel(x_ref, a_ref, w_ref, g_ref, o_ref):
    o_ref[...] = x_ref[...] + _rms(_dot(a_ref[...].astype(BF16), w_ref[...]), g_ref[...])


def _out_proj(x, a, w_bf, g, *, tm):
    m, d = x.shape
    row = pl.BlockSpec((tm, d), lambda i: (i, 0))
    return pl.pallas_call(
        _out_proj_kernel,
        grid=(m // tm,),
        in_specs=[row, pl.BlockSpec((tm, a.shape[1]), lambda i: (i, 0)), pl.BlockSpec(w_bf.shape, lambda i: (0, 0)),
                  pl.BlockSpec((1, d), lambda i: (0, 0))],
        out_specs=row,
        out_shape=jax.ShapeDtypeStruct((m, d), F32),
        compiler_params=_cparams(("parallel",)),
        name="out_proj",
    )(x, a, w_bf, g.reshape(1, d))


def _ffn_kernel(x_ref, gpre_ref, wug_ref, wuv_ref, cwg_ref, cwv_ref, cbg_ref, cbv_ref, wd_ref, hg_ref, hv_ref,
                gpost_ref, o_ref, tg_ref, tv_ref, xn_ref, acc_ref, halo_g_ref, halo_v_ref, *, shift,
                blocks_per_seq):
    i = pl.program_id(0)
    j = pl.program_id(1)
    tm = x_ref.shape[0]

    @pl.when(j == 0)
    def _():
        xn_ref[...] = _rms(x_ref[...], gpre_ref[...]).astype(BF16)
        acc_ref[...] = jnp.zeros_like(acc_ref)

    seq_start = (i % blocks_per_seq) == 0

    def conv(w_ref, cw_ref, cb_ref, hist_ref, halo_ref, tail_ref):
        u = _dot(xn_ref[...], w_ref[...])
        prev = jnp.where(seq_start, hist_ref[0], halo_ref[j])
        if shift == 1:
            rows = lax.broadcasted_iota(jnp.int32, u.shape, 0)
            u1 = jnp.where(rows == 0, prev[1:2], pltpu.roll(u, 1, 0))
            u2 = jnp.where(rows == 0, prev[0:1], jnp.where(rows == 1, prev[1:2], pltpu.roll(u, 2, 0)))
        else:
            ue = jnp.concatenate([prev, u], axis=0)
            u2 = ue[:tm]
            u1 = ue[shift:shift + tm]
        tail = u[tm - 2 * shift:]
        halo_ref[j] = tail
        tail_ref[0] = tail
        cw = cw_ref[...]
        return cb_ref[...] + cw[0:1] * u2 + cw[1:2] * u1 + cw[2:3] * u

    gate = conv(wug_ref, cwg_ref, cbg_ref, hg_ref, halo_g_ref, tg_ref)
    val = conv(wuv_ref, cwv_ref, cbv_ref, hv_ref, halo_v_ref, tv_ref)
    acc_ref[...] += _dot((_silu(gate) * val).astype(BF16), wd_ref[...])

    @pl.when(j == pl.num_programs(1) - 1)
    def _():
        o_ref[...] = x_ref[...] + _rms(acc_ref[...], gpost_ref[...])


def _conv_ffn(x, g_pre, wu_bf, conv_w, conv_b, wd_bf, hist, g_post, *, tm, tf, shift, blocks_per_seq):
    m, d = x.shape
    f = wd_bf.shape[0]
    nj = f // tf
    nblk = m // tm
    halo = 2 * shift
    up = lambda off: pl.BlockSpec((d, tf), lambda i, j: (0, off + j))
    cw = lambda off: pl.BlockSpec((CONV_WIDTH, tf), lambda i, j: (0, off + j))
    cb = lambda off: pl.BlockSpec((1, tf), lambda i, j: (0, off + j))
    hs = lambda off: pl.BlockSpec((1, halo, tf), lambda i, j: (i // blocks_per_seq, 0, off + j))
    vec = pl.BlockSpec((1, d), lambda i, j: (0, 0))
    row = pl.BlockSpec((tm, d), lambda i, j: (i, 0))
    tail = pl.BlockSpec((1, halo, tf), lambda i, j: (i, 0, j))
    return pl.pallas_call(
        functools.partial(_ffn_kernel, shift=shift, blocks_per_seq=blocks_per_seq),
        grid=(nblk, nj),
        in_specs=[row, vec, up(0), up(nj), cw(0), cw(nj), cb(0), cb(nj),
                  pl.BlockSpec((tf, d), lambda i, j: (j, 0)), hs(0), hs(nj), vec],
        out_specs=[row, tail, tail],
        out_shape=[jax.ShapeDtypeStruct((m, d), F32),
                   jax.ShapeDtypeStruct((nblk, halo, f), F32),
                   jax.ShapeDtypeStruct((nblk, halo, f), F32)],
        scratch_shapes=[pltpu.VMEM((tm, d), BF16), pltpu.VMEM((tm, d), F32),
                        pltpu.VMEM((nj, halo, tf), F32), pltpu.VMEM((nj, halo, tf), F32)],
        compiler_params=_cparams(("arbitrary", "arbitrary")),
        name="conv_ffn",
    )(x, g_pre.reshape(1, d), wu_bf, wu_bf, conv_w, conv_w, conv_b.reshape(1, -1), conv_b.reshape(1, -1), wd_bf,
      hist, hist, g_post.reshape(1, d))


def _rope_tables(pos):
    inv = 1.0 / (ROPE_THETA ** (jnp.arange(0, HEAD_DIM, 2, dtype=F32) / HEAD_DIM))
    ang = pos.astype(F32)[:, None] * inv[None, :]
    ang = jnp.concatenate([ang, ang], axis=-1)
    cos, sin = jnp.cos(ang), jnp.sin(ang)
    first_half = jnp.arange(HEAD_DIM) < HEAD_DIM // 2
    sin_a = jnp.where(first_half[None], -sin, 0.0)
    sin_b = jnp.where(first_half[None], 0.0, sin)
    rep = LANES // HEAD_DIM
    return tuple(jnp.tile(t, (1, rep)) for t in (cos, sin_a, sin_b))


def _tile_for(n, target):
    t = min(n, target)
    assert n % t == 0, (n, t)
    return t


def kernel(x_prompt, x_sample, mem_prompt, state_ret, cache_diff_k, cache_diff_v, page_table, cache_mem_k, cache_mem_v, state_conv, g_mix_pre, g_mix_post, w_in, ret_norm_g, lam_q1, lam_k1, lam_q2, lam_k2, diff_norm_g, w_out, g_mem_pre, g_mem_post, g_mem_kv, w_mq, w_mk, w_mv, w_mo, g_ffn_pre, g_ffn_post, w_up, conv_w, conv_b, w_down):
    B, S, D = x_prompt.shape
    DB, DS, _ = x_sample.shape
    depth = w_in.shape[0]
    n_pool, page = cache_diff_k.shape[1], cache_diff_k.shape[2]
    past = page_table.shape[1] * page
    n_mem = mem_prompt.shape[1]
    two_f = w_up.shape[2]
    mem_hd = D // N_MEM_HEADS
    mp, ms = B * S, DB * DS

    tm_p = _tile_for(S, 512)
    tm_big = _tile_for(S, 1024)
    tile_attn = _tile_for(S, 512)
    tf = 256
    assert (two_f // 2) % tf == 0
    assert mem_hd == 2 * LANES

    tabs_p = _rope_tables(jnp.arange(S, dtype=jnp.int32))
    tabs_s = _rope_tables(jnp.tile(past + jnp.arange(DS, dtype=jnp.int32), DB))
    no_tabs_mem = tuple(jnp.zeros((B * n_mem, LANES), F32) for _ in range(3))
    no_tabs_s = tuple(jnp.zeros((ms, LANES), F32) for _ in range(3))

    kc = cache_diff_k.reshape(depth, n_pool, page * N_DIFF_HEADS, DIFF_DV)
    vc = cache_diff_v.reshape(depth, n_pool, page * N_DIFF_HEADS, DIFF_DV)

    def mem_rows(c):
        c = c.reshape(depth, DB, n_mem, N_MEM_HEADS, 2, LANES).transpose(0, 1, 2, 4, 3, 5)
        return c.reshape(depth, DB, n_mem * 2 * N_MEM_HEADS, LANES)

    mkc = mem_rows(cache_mem_k)
    mvc = mem_rows(cache_mem_v)

    xp = x_prompt.reshape(mp, D)
    xs = x_sample.reshape(ms, D)
    mem2d = mem_prompt.reshape(B * n_mem, D)
    zero_hist = jnp.zeros((B, 2, two_f), F32)
    ret_pad = 8

    outs = {k: [] for k in ("p_ret", "p_dk", "p_dv", "p_mk", "p_mv", "p_conv", "s_ret", "s_dk", "s_dv", "s_conv")}
    rope_secs = (0, 1, 4, 5)
    for l in range(depth):
        lam_init = 0.8 - 0.6 * math.exp(-0.3 * l)
        lam_params = [t[l].reshape(1, HEAD_DIM) for t in (lam_q1, lam_k1, lam_q2, lam_k2)]
        w_in_bf = w_in[l].astype(BF16)
        w_out_bf = w_out[l].astype(BF16)
        w_mq_bf = w_mq[l].astype(BF16)
        w_mo_bf = w_mo[l].astype(BF16)
        w_mkv_bf = jnp.concatenate([w_mk[l], w_mv[l]], axis=1).astype(BF16)
        w_up_bf = w_up[l].astype(BF16)
        w_down_bf = w_down[l].astype(BF16)

        z, dk, dv = _project(xp, g_mix_pre[l], w_in_bf, tabs_p, tm=tm_big, tn=SECTION, rope_sections=rope_secs,
                             k_section=1, n_main=5)
        ro, s_fin = _retention_prompt(z, ret_norm_g[l], B, S, rows_per_step=tm_p)
        do = _diff_prompt(z, dk, dv, lam_params, diff_norm_g[l], lam_init, B, S, tile=tile_attn, heads=4)
        xp = _merge(xp, z, ro, do, w_out_bf, g_mix_post[l], tm=tm_p)
        outs["p_ret"].append(s_fin)
        outs["p_dk"].append(dk.reshape(B, S, N_DIFF_HEADS, DIFF_DV))
        outs["p_dv"].append(dv.reshape(B, S, N_DIFF_HEADS, DIFF_DV))

        zs, dks, dvs = _project(xs, g_mix_pre[l], w_in_bf, tabs_s, tm=ms, tn=SECTION, rope_sections=rope_secs,
                                k_section=1, n_main=5)

        def heads_first(t):
            t = t.reshape(DB, DS, N_RET_HEADS, HEAD_DIM).transpose(0, 2, 1, 3)
            return jnp.pad(t, ((0, 0), (0, 0), (0, ret_pad - DS), (0, 0)))

        ro_s, s_new = _retention_sample(heads_first(zs[:, :SECTION]), heads_first(zs[:, SECTION:2 * SECTION]),
                                        heads_first(zs[:, 2 * SECTION:3 * SECTION]), state_ret[l], ret_norm_g[l],
                                        DS, bb=8)
        ro_s = ro_s[:, :, :DS].transpose(0, 2, 1, 3).reshape(ms, SECTION)

        dq = zs[:, 4 * SECTION:].reshape(DB, DS, N_DIFF_HEADS, 2, HEAD_DIM) * (HEAD_DIM ** -0.5)
        dq = dq.transpose(0, 3, 1, 2, 4)
        zeros = jnp.zeros_like(dq[:, 0])
        qbig = jnp.stack([jnp.concatenate([dq[:, 0], zeros], -1), jnp.concatenate([zeros, dq[:, 1]], -1)], axis=1)
        qbig = qbig.reshape(DB, 2 * DS * N_DIFF_HEADS, DIFF_DV)
        g_rows = jnp.tile(diff_norm_g[l], (DS, 1))
        do_s = _diff_sample(qbig, dks.reshape(DB, DS * N_DIFF_HEADS, DIFF_DV),
                            dvs.reshape(DB, DS * N_DIFF_HEADS, DIFF_DV), kc, vc, page_table, l, lam_params, g_rows,
                            lam_init)
        xs = _merge(xs, zs, ro_s, do_s.reshape(ms, N_DIFF_HEADS * DIFF_DV), w_out_bf, g_mix_post[l], tm=ms)
        outs["s_ret"].append(s_new)
        outs["s_dk"].append(dks.reshape(DB, DS, N_DIFF_HEADS, DIFF_DV))
        outs["s_dv"].append(dvs.reshape(DB, DS, N_DIFF_HEADS, DIFF_DV))

        mkv, = _project(mem2d, g_mem_kv[l], w_mkv_bf, no_tabs_mem, tm=B * n_mem, tn=SECTION)
        mk = mkv[:, :D].reshape(B, n_mem, D)
        mv = mkv[:, D:].reshape(B, n_mem, D)
        xp = _mem_prompt(xp, g_mem_pre[l], w_mq_bf, mk, mv, w_mo_bf, g_mem_post[l], S, tm=tm_p)
        qs, = _project(xs, g_mem_pre[l], w_mq_bf, no_tabs_s, tm=ms, tn=SECTION)
        q_rows = qs.reshape(DB, DS * N_MEM_HEADS, 2, LANES).transpose(0, 2, 1, 3)
        om = _mem_sample(q_rows.reshape(DB, 2 * DS * N_MEM_HEADS, LANES), mkc, mvc, l, bb=4)
        xs = _out_proj(xs, om.reshape(ms, D), w_mo_bf, g_mem_post[l], tm=ms)
        outs["p_mk"].append(mk.reshape(B, n_mem, N_MEM_HEADS, mem_hd))
        outs["p_mv"].append(mv.reshape(B, n_mem, N_MEM_HEADS, mem_hd))

        xp, tg, tv = _conv_ffn(xp, g_ffn_pre[l], w_up_bf, conv_w[l], conv_b[l], w_down_bf, zero_hist, g_ffn_post[l],
                               tm=tm_big, tf=tf, shift=1, blocks_per_seq=S // tm_big)
        last = jnp.arange(B) * (S // tm_big) + (S // tm_big - 1)
        outs["p_conv"].append(jnp.concatenate([tg[last], tv[last]], axis=-1))
        xs_t = xs.reshape(DB, DS, D).transpose(1, 0, 2).reshape(ms, D)
        hist_s = state_conv[l].transpose(1, 0, 2).reshape(1, 2 * DB, two_f)
        xs_t, tg, tv = _conv_ffn(xs_t, g_ffn_pre[l], w_up_bf, conv_w[l], conv_b[l], w_down_bf, hist_s, g_ffn_post[l],
                                 tm=ms, tf=tf, shift=DB, blocks_per_seq=1)
        xs = xs_t.reshape(DS, DB, D).transpose(1, 0, 2).reshape(ms, D)
        outs["s_conv"].append(jnp.concatenate([tg[0], tv[0]], axis=-1).reshape(2, DB, two_f).transpose(1, 0, 2))

    st = lambda k: jnp.stack(outs[k])
    return (xp.reshape(B, S, D), xs.reshape(DB, DS, D), st("p_ret"), st("p_dk"), st("p_dv"), st("p_mk"), st("p_mv"),
            st("p_conv"), st("s_ret"), st("s_dk"), st("s_dv"), st("s_conv"))
```

```python
import functools
import math

import jax
import jax.numpy as jnp
from jax import lax
from jax.experimental import pallas as pl
from jax.experimental.pallas import tpu as pltpu

F32 = jnp.float32
BF16 = jnp.bfloat16

HEAD_DIM = 64
N_RET_HEADS = 8
N_DIFF_HEADS = 4
DIFF_DV = 2 * HEAD_DIM
SECTION = 512
N_SECTIONS = 7
N_MEM_HEADS = 4
CONV_WIDTH = 3
ROPE_THETA = 10000.0
EPS = 1e-6
NEG_BIG = -1e30
LOG2_E = math.log2(math.e)
RET_CHUNK = 128

LANES = 128
BF16_SUBLANES = 16
ONES_ROWS = BF16_SUBLANES
V7X_VMEM_BYTES = 64 * 1024 * 1024
VMEM_LIMIT = (V7X_VMEM_BYTES * 3) // 4


def _cparams(semantics):
    return pltpu.CompilerParams(dimension_semantics=semantics, vmem_limit_bytes=VMEM_LIMIT)


def _rms(x, g):
    return x * lax.rsqrt(jnp.mean(x * x, axis=-1, keepdims=True) + EPS) * g


def _dot(a, b):
    return jnp.dot(a, b, preferred_element_type=F32)


def _dot_nt(a, b):
    return lax.dot_general(a, b, (((1,), (1,)), ((), ())), preferred_element_type=F32)


def _silu(x):
    return x * (1.0 / (1.0 + jnp.exp(-x)))


def _proj_kernel(x_ref, g_ref, w_ref, cos_ref, sa_ref, sb_ref, *out_and_scratch, rope_sections, k_section,
                 n_main):
    outs, xn_ref = out_and_scratch[:-1], out_and_scratch[-1]
    j = pl.program_id(1)

    @pl.when(j == 0)
    def _():
        xn_ref[...] = _rms(x_ref[...], g_ref[...]).astype(BF16)

    acc = _dot(xn_ref[...], w_ref[...])
    n_sections = n_main + len(outs) - 1
    for sec in range(n_sections):
        o_ref = outs[0] if sec < n_main else outs[sec - n_main + 1]

        @pl.when(j == sec)
        def _(o_ref=o_ref, sec=sec):
            if sec in rope_sections:
                scale = HEAD_DIM ** -0.5 if sec == k_section else 1.0
                c, sa, sb = cos_ref[...], sa_ref[...], sb_ref[...]
                for t in range(acc.shape[1] // LANES):
                    a = acc[:, t * LANES:(t + 1) * LANES]
                    r = a * c + pltpu.roll(a, LANES - HEAD_DIM // 2, 1) * sa + pltpu.roll(a, HEAD_DIM // 2, 1) * sb
                    o_ref[:, t * LANES:(t + 1) * LANES] = r * scale if scale != 1.0 else r
            else:
                o_ref[...] = acc


def _project(x, g, w_bf, tabs, *, tm, tn, rope_sections=(), k_section=-1, n_main=None):
    m, d = x.shape
    n = w_bf.shape[1]
    n_sec = n // tn
    n_main = n_sec if n_main is None else n_main
    cos, sa, sb = tabs
    tab_blocks = cos.shape[0] // tm
    tab_spec = pl.BlockSpec((tm, LANES), lambda i, j: (i % tab_blocks, 0))
    out_shape = [jax.ShapeDtypeStruct((m, n_main * tn), F32)]
    out_specs = [pl.BlockSpec((tm, tn), lambda i, j: (i, jnp.minimum(j, n_main - 1)))]
    for _ in range(n_sec - n_main):
        out_shape.append(jax.ShapeDtypeStruct((m, tn), F32))
        out_specs.append(pl.BlockSpec((tm, tn), lambda i, j: (i, 0)))
    kern = functools.partial(_proj_kernel, rope_sections=tuple(rope_sections), k_section=k_section, n_main=n_main)
    return pl.pallas_call(
        kern,
        grid=(m // tm, n_sec),
        in_specs=[pl.BlockSpec((tm, d), lambda i, j: (i, 0)),
                  pl.BlockSpec((1, d), lambda i, j: (0, 0)),
                  pl.BlockSpec((d, tn), lambda i, j: (0, j)),
                  tab_spec, tab_spec, tab_spec],
        out_specs=out_specs,
        out_shape=out_shape,
        scratch_shapes=[pltpu.VMEM((tm, d), BF16)],
        compiler_params=_cparams(("parallel", "arbitrary")),
        name="project",
    )(x, g.reshape(1, d), w_bf, cos, sa, sb)


def _ret_prompt_kernel(q_ref, k_ref, v_ref, dec_ref, rw_ref, ww_ref, cd_ref, g_ref, o_ref, sfin_ref, s_ref, *,
                       n_chunks):
    r = pl.program_id(1)

    @pl.when(r == 0)
    def _():
        s_ref[...] = jnp.zeros_like(s_ref)

    lane = lax.broadcasted_iota(jnp.int32, (RET_CHUNK, LANES), 1)
    first = lane < HEAD_DIM
    srow = lax.broadcasted_iota(jnp.int32, (LANES, LANES), 0)
    scol = lax.broadcasted_iota(jnp.int32, (LANES, LANES), 1)
    same_head = (srow < HEAD_DIM) == (scol < HEAD_DIM)

    def half_mean(t):
        tot = jnp.sum(t, axis=-1, keepdims=True)
        lo = jnp.sum(jnp.where(first, t, 0.0), axis=-1, keepdims=True)
        return jnp.where(first, lo, tot - lo) * (1.0 / HEAD_DIM)

    def chunk(c, carry):
        row = pl.multiple_of(c * RET_CHUNK, RET_CHUNK)
        rows = pl.ds(row, RET_CHUNK)
        for p in range(N_RET_HEADS // 2):
            cols = slice(p * LANES, (p + 1) * LANES)
            q = q_ref[rows, cols]
            k = k_ref[rows, cols]
            vb = v_ref[rows, cols].astype(BF16)
            kb = k.astype(BF16)
            s = s_ref[p]
            sc_a = (_dot_nt(jnp.where(first, q, 0.0).astype(BF16), kb) * dec_ref[2 * p]).astype(BF16)
            sc_b = (_dot_nt(jnp.where(first, 0.0, q).astype(BF16), kb) * dec_ref[2 * p + 1]).astype(BF16)
            o = jnp.where(first, _dot(sc_a, vb), _dot(sc_b, vb)) + _dot(q.astype(BF16), s.astype(BF16)) * rw_ref[p]
            kw_t = (k * ww_ref[p]).T.astype(BF16)
            s_ref[p] = s * cd_ref[p] + jnp.where(same_head, _dot(kw_t, vb), 0.0)
            d = o - half_mean(o)
            o_ref[rows, cols] = d * lax.rsqrt(half_mean(d * d) + EPS) * g_ref[p]
        return carry

    lax.fori_loop(0, n_chunks, chunk, 0)

    @pl.when(r == pl.num_programs(1) - 1)
    def _():
        for h in range(N_RET_HEADS):
            lo = (h % 2) * HEAD_DIM
            sfin_ref[0, h] = s_ref[h // 2][lo:lo + HEAD_DIM, lo:lo + HEAD_DIM]


def _retention_tables(length):
    log_g = jnp.log1p(-jnp.exp2(-5.0 - jnp.arange(N_RET_HEADS, dtype=F32)))
    n = jnp.arange(length, dtype=F32)
    diff = n[:, None] - n[None, :]
    dec = jnp.where((diff >= 0)[None], jnp.exp(log_g[:, None, None] * jnp.maximum(diff, 0.0)[None]), 0.0)
    read_w = jnp.exp(log_g[:, None] * (n + 1.0)[None])
    write_w = jnp.exp(log_g[:, None] * (length - 1.0 - n)[None])
    chunk_decay = jnp.exp(log_g * length)
    return dec, read_w, write_w, chunk_decay


def _retention_prompt(z, ret_g, batch, seq, *, rows_per_step):
    m = z.shape[0]
    nr = seq // rows_per_step
    dec, read_w, write_w, chunk_decay = _retention_tables(RET_CHUNK)
    n_pairs = N_RET_HEADS // 2

    def per_lane(t):
        t = jnp.broadcast_to(t[:, :, None], (N_RET_HEADS, t.shape[1], HEAD_DIM))
        return t.reshape(n_pairs, 2, t.shape[1], HEAD_DIM).transpose(0, 2, 1, 3).reshape(n_pairs, t.shape[1], LANES)

    rw = per_lane(read_w)
    ww = per_lane(write_w)
    cd = jnp.repeat(per_lane(jnp.broadcast_to(chunk_decay[:, None], (N_RET_HEADS, HEAD_DIM))), 2, axis=1)
    full3 = lambda shape: pl.BlockSpec(shape, lambda b, r: (0, 0, 0))
    sec = lambda c: pl.BlockSpec((rows_per_step, SECTION), lambda b, r: (b * nr + r, c))
    return pl.pallas_call(
        functools.partial(_ret_prompt_kernel, n_chunks=rows_per_step // RET_CHUNK),
        grid=(batch, nr),
        in_specs=[sec(0), sec(1), sec(2),
                  full3((N_RET_HEADS, RET_CHUNK, RET_CHUNK)),
                  full3((n_pairs, RET_CHUNK, LANES)),
                  full3((n_pairs, RET_CHUNK, LANES)),
                  full3((n_pairs, LANES, LANES)),
                  full3((n_pairs, 1, LANES))],
        out_specs=[pl.BlockSpec((rows_per_step, SECTION), lambda b, r: (b * nr + r, 0)),
                   pl.BlockSpec((1, N_RET_HEADS, HEAD_DIM, HEAD_DIM), lambda b, r: (b, 0, 0, 0))],
        out_shape=[jax.ShapeDtypeStruct((m, SECTION), F32),
                   jax.ShapeDtypeStruct((batch, N_RET_HEADS, HEAD_DIM, HEAD_DIM), F32)],
        scratch_shapes=[pltpu.VMEM((n_pairs, LANES, LANES), F32)],
        compiler_params=_cparams(("parallel", "arbitrary")),
        name="retention_prompt",
    )(z, z, z, dec, rw, ww, cd, ret_g.reshape(n_pairs, 1, LANES))


def _ret_sample_kernel(q_ref, k_ref, v_ref, s_ref, dec_ref, rw_ref, ww_ref, cd_ref, g_ref, o_ref, snew_ref):
    n = q_ref.shape[0] * q_ref.shape[1]
    length = q_ref.shape[2]
    q = q_ref[...].reshape(n, length, HEAD_DIM)
    k = k_ref[...].reshape(n, length, HEAD_DIM)
    vb = v_ref[...].reshape(n, length, HEAD_DIM).astype(BF16)
    s = s_ref[...].reshape(n, HEAD_DIM, HEAD_DIM)
    qb = q.astype(BF16)
    scores = jnp.einsum('nqd,nkd->nqk', qb, k.astype(BF16), preferred_element_type=F32) * dec_ref[...]
    o = jnp.einsum('nqk,nke->nqe', scores.astype(BF16), vb, preferred_element_type=F32)
    o = o + jnp.einsum('nqd,nde->nqe', qb, s.astype(BF16), preferred_element_type=F32) * rw_ref[...]
    kw = (k * ww_ref[...]).astype(BF16)
    s_new = s * cd_ref[...] + jnp.einsum('nld,nle->nde', kw, vb, preferred_element_type=F32)
    mu = jnp.mean(o, axis=-1, keepdims=True)
    var = jnp.mean(jnp.square(o - mu), axis=-1, keepdims=True)
    o_ref[...] = ((o - mu) * lax.rsqrt(var + EPS) * g_ref[...]).reshape(o_ref.shape)
    snew_ref[...] = s_new.reshape(snew_ref.shape)


def _retention_sample(q, k, v, state, ret_g, n_tokens, *, bb):
    db, nh, length, _ = q.shape
    dec, read_w, write_w, chunk_decay = _retention_tables(n_tokens)
    pad = length - n_tokens
    dec = jnp.pad(dec, ((0, 0), (0, pad), (0, pad)))
    read_w = jnp.pad(read_w, ((0, 0), (0, pad)))
    write_w = jnp.pad(write_w, ((0, 0), (0, pad)))
    n = bb * nh
    tile = lambda t: jnp.tile(t, (bb,) + (1,) * (t.ndim - 1))
    dec_t = tile(dec)
    rw_t = tile(jnp.broadcast_to(read_w[:, :, None], (nh, length, HEAD_DIM)))
    ww_t = tile(jnp.broadcast_to(write_w[:, :, None], (nh, length, HEAD_DIM)))
    cd_t = tile(jnp.broadcast_to(chunk_decay[:, None, None], (nh, HEAD_DIM, HEAD_DIM)))
    g_t = tile(jnp.broadcast_to(ret_g[:, None, :], (nh, length, HEAD_DIM)))
    blk = pl.BlockSpec((bb, nh, length, HEAD_DIM), lambda i: (i, 0, 0, 0))
    sblk = pl.BlockSpec((bb, nh, HEAD_DIM, HEAD_DIM), lambda i: (i, 0, 0, 0))
    full = lambda a: pl.BlockSpec(a.shape, lambda i: (0,) * a.ndim)
    return pl.pallas_call(
        _ret_sample_kernel,
        grid=(db // bb,),
        in_specs=[blk, blk, blk, sblk, full(dec_t), full(rw_t), full(ww_t), full(cd_t), full(g_t)],
        out_specs=[blk, sblk],
        out_shape=[jax.ShapeDtypeStruct(q.shape, F32), jax.ShapeDtypeStruct(state.shape, F32)],
        compiler_params=_cparams(("parallel",)),
        name="retention_sample",
    )(q, k, v, state, dec_t, rw_t, ww_t, cd_t, g_t)


def _lambda(lq1_ref, lk1_ref, lq2_ref, lk2_ref, lam_init):
    e1 = jnp.exp(jnp.sum(lq1_ref[...] * lk1_ref[...], axis=-1, keepdims=True))
    e2 = jnp.exp(jnp.sum(lq2_ref[...] * lk2_ref[...], axis=-1, keepdims=True))
    return e1 - e2 + lam_init


def _diff_prompt_kernel(qi_ref, ki_ref, q_ref, k_ref, v_ref, lq1_ref, lk1_ref, lq2_ref, lk2_ref, g_ref, o_ref,
                        m_ref, acc_ref, qm_ref, *, lam_init, heads):
    p = pl.program_id(2)
    qi = qi_ref[p]
    ki = ki_ref[p]
    tk = k_ref.shape[0]

    @pl.when(ki == 0)
    def _():
        m_ref[...] = jnp.full_like(m_ref, NEG_BIG)
        acc_ref[...] = jnp.zeros_like(acc_ref)
        lane = lax.broadcasted_iota(jnp.int32, (q_ref.shape[0], DIFF_DV), 1)
        for h in range(heads):
            q = q_ref[:, h * DIFF_DV:(h + 1) * DIFF_DV] * (HEAD_DIM ** -0.5 * LOG2_E)
            qm_ref[2 * h] = jnp.where(lane < HEAD_DIM, q, 0.0).astype(BF16)
            qm_ref[2 * h + 1] = jnp.where(lane < HEAD_DIM, 0.0, q).astype(BF16)

    def step(masked):
        ones = jnp.ones((ONES_ROWS, tk), BF16)
        kb = [None] * heads

        def scores(idx):
            h = idx // 2
            if kb[h] is None:
                kb[h] = k_ref[:, h * DIFF_DV:(h + 1) * DIFF_DV].astype(BF16)
            return _dot_nt(kb[h], qm_ref[idx])

        def accumulate(idx, s, v_ext):
            if masked:
                key = lax.broadcasted_iota(jnp.int32, s.shape, 0)
                qry = lax.broadcasted_iota(jnp.int32, s.shape, 1)
                s = jnp.where(key <= qry, s, NEG_BIG)
            m_old = m_ref[idx]
            m_new = jnp.maximum(m_old, jnp.max(s, axis=0, keepdims=True))
            alpha = jnp.exp2(m_old - m_new)
            e = jnp.exp2(s - m_new).astype(BF16)
            acc_ref[idx] = alpha * acc_ref[idx] + _dot(v_ext, e)
            m_ref[idx] = m_new

        n = 2 * heads
        s_next = scores(0)
        v_ext = None
        for idx in range(n):
            s_cur = s_next
            if idx + 1 < n:
                s_next = scores(idx + 1)
            if idx % 2 == 0:
                hcols = slice((idx // 2) * DIFF_DV, (idx // 2 + 1) * DIFF_DV)
                v_ext = jnp.concatenate([v_ref[:, hcols].T.astype(BF16), ones], axis=0)
            accumulate(idx, s_cur, v_ext)

    @pl.when(ki < qi)
    def _():
        step(False)

    @pl.when(ki == qi)
    def _():
        step(True)
        lam = _lambda(lq1_ref, lk1_ref, lq2_ref, lk2_ref, lam_init)
        for h in range(heads):
            a1 = acc_ref[2 * h]
            a2 = acc_ref[2 * h + 1]
            o_t = (a1[:DIFF_DV] / a1[DIFF_DV:DIFF_DV + 1] - lam * (a2[:DIFF_DV] / a2[DIFF_DV:DIFF_DV + 1]))
            o_ref[:, h * DIFF_DV:(h + 1) * DIFF_DV] = _rms(o_t.T, g_ref[h]) * (1.0 - lam_init)


def _diff_prompt(z, dk, dv, lam_params, diff_g, lam_init, batch, seq, *, tile, heads):
    m = z.shape[0]
    nt = seq // tile
    width = heads * DIFF_DV
    groups = N_DIFF_HEADS // heads
    pairs = [(qi, ki) for qi in range(nt) for ki in range(qi + 1)]
    qi_tab = jnp.asarray([p[0] for p in pairs], jnp.int32)
    ki_tab = jnp.asarray([p[1] for p in pairs], jnp.int32)
    q_col0 = 4 * SECTION // width
    lam_spec = pl.BlockSpec((1, HEAD_DIM), lambda b, h, p, qi, ki: (0, 0))
    grid_spec = pltpu.PrefetchScalarGridSpec(
        num_scalar_prefetch=2,
        grid=(batch, groups, len(pairs)),
        in_specs=[pl.BlockSpec((tile, width), lambda b, h, p, qi, ki: (b * nt + qi[p], q_col0 + h)),
                  pl.BlockSpec((tile, width), lambda b, h, p, qi, ki: (b * nt + ki[p], h)),
                  pl.BlockSpec((tile, width), lambda b, h, p, qi, ki: (b * nt + ki[p], h)),
                  lam_spec, lam_spec, lam_spec, lam_spec,
                  pl.BlockSpec((heads, 1, DIFF_DV), lambda b, h, p, qi, ki: (h, 0, 0))],
        out_specs=pl.BlockSpec((tile, width), lambda b, h, p, qi, ki: (b * nt + qi[p], h)),
        scratch_shapes=[pltpu.VMEM((2 * heads, 1, tile), F32),
                        pltpu.VMEM((2 * heads, DIFF_DV + ONES_ROWS, tile), F32),
                        pltpu.VMEM((2 * heads, tile, DIFF_DV), BF16)],
    )
    return pl.pallas_call(
        functools.partial(_diff_prompt_kernel, lam_init=lam_init, heads=heads),
        grid_spec=grid_spec,
        out_shape=jax.ShapeDtypeStruct((m, N_DIFF_HEADS * DIFF_DV), F32),
        compiler_params=_cparams(("parallel", "parallel", "arbitrary")),
        name="diff_attn_prompt",
    )(qi_tab, ki_tab, z, dk, dv, *lam_params, diff_g.reshape(N_DIFF_HEADS, 1, DIFF_DV))


def _diff_sample_kernel(pt_ref, q_ref, kn_ref, vn_ref, *rest, n_pages, lam_init):
    k_pages = rest[:n_pages]
    v_pages = rest[n_pages:2 * n_pages]
    g_ref, lq1_ref, lk1_ref, lq2_ref, lk2_ref, o_ref, s_ref = rest[2 * n_pages:]
    n_rows = q_ref.shape[1]
    half = n_rows // 2
    page_rows = k_pages[0].shape[0]
    q = q_ref[0].astype(BF16)

    row_h = lax.broadcasted_iota(jnp.int32, (n_rows, page_rows), 0) % N_DIFF_HEADS
    col_h = lax.broadcasted_iota(jnp.int32, (n_rows, page_rows), 1) % N_DIFF_HEADS
    same_head = row_h == col_h
    m = jnp.full((n_rows, 1), NEG_BIG, F32)
    for p in range(n_pages):
        s = jnp.where(same_head, _dot_nt(q, k_pages[p][...].astype(BF16)), NEG_BIG)
        s_ref[:, p * page_rows:(p + 1) * page_rows] = s
        m = jnp.maximum(m, jnp.max(s, axis=-1, keepdims=True))

    n_new = kn_ref.shape[1]
    r = lax.broadcasted_iota(jnp.int32, (n_rows, n_new), 0)
    c = lax.broadcasted_iota(jnp.int32, (n_rows, n_new), 1)
    ok = (r % N_DIFF_HEADS == c % N_DIFF_HEADS) & (c // N_DIFF_HEADS <= (r % half) // N_DIFF_HEADS)
    s_new = jnp.where(ok, _dot_nt(q, kn_ref[0].astype(BF16)), NEG_BIG)
    m = jnp.maximum(m, jnp.max(s_new, axis=-1, keepdims=True))

    e_new = jnp.exp(s_new - m)
    l = jnp.sum(e_new, axis=-1, keepdims=True)
    for p in range(n_pages):
        cols = slice(p * page_rows, (p + 1) * page_rows)
        e = jnp.exp(s_ref[:, cols] - m)
        s_ref[:, cols] = e
        l = l + jnp.sum(e, axis=-1, keepdims=True)

    lam = _lambda(lq1_ref, lk1_ref, lq2_ref, lk2_ref, lam_init)
    inv = 1.0 / l
    w1 = inv[:half]
    w2 = lam * inv[half:]
    a_new = (e_new[:half] * w1 - e_new[half:] * w2).astype(BF16)
    o = _dot(a_new, vn_ref[0].astype(BF16))
    for p in range(n_pages):
        cols = slice(p * page_rows, (p + 1) * page_rows)
        a = (s_ref[:half, cols] * w1 - s_ref[half:, cols] * w2).astype(BF16)
        o = o + _dot(a, v_pages[p][...].astype(BF16))
    o_ref[0] = _rms(o, g_ref[...]) * (1.0 - lam_init)


def _diff_sample(qbig, k_new, v_new, k_cache, v_cache, page_table, layer, lam_params, g_rows, lam_init):
    db, n_rows, _ = qbig.shape
    n_pages = page_table.shape[1]
    page_rows = k_cache.shape[2]
    n_new = k_new.shape[1]

    def page_spec(p):
        return pl.BlockSpec((None, None, page_rows, LANES), lambda b, pt: (layer, pt[b * n_pages + p], 0, 0))

    small = lambda rows: pl.BlockSpec((1, rows, LANES), lambda b, pt: (b, 0, 0))
    lam_spec = pl.BlockSpec((1, HEAD_DIM), lambda b, pt: (0, 0))
    grid_spec = pltpu.PrefetchScalarGridSpec(
        num_scalar_prefetch=1,
        grid=(db,),
        in_specs=[small(n_rows), small(n_new), small(n_new)]
        + [page_spec(p) for p in range(n_pages)] * 2
        + [pl.BlockSpec((n_rows // 2, LANES), lambda b, pt: (0, 0)), lam_spec, lam_spec, lam_spec, lam_spec],
        out_specs=small(n_rows // 2),
        scratch_shapes=[pltpu.VMEM((n_rows, n_pages * page_rows), F32)],
    )
    return pl.pallas_call(
        functools.partial(_diff_sample_kernel, n_pages=n_pages, lam_init=lam_init),
        grid_spec=grid_spec,
        out_shape=jax.ShapeDtypeStruct((db, n_rows // 2, LANES), F32),
        compiler_params=_cparams(("parallel",)),
        name="diff_attn_sample",
    )(page_table.reshape(-1), qbig, k_new, v_new, *([k_cache] * n_pages), *([v_cache] * n_pages), g_rows,
      *lam_params)


def _merge_kernel(x_ref, rg_ref, ro_ref, do_ref, w_ref, g_ref, o_ref):
    a = jnp.concatenate([_silu(rg_ref[...]) * ro_ref[...], do_ref[...]], axis=-1).astype(BF16)
    o_ref[...] = x_ref[...] + _rms(_dot(a, w_ref[...]), g_ref[...])


def _merge(x, z, ro, do, w_bf, g, *, tm):
    m, d = x.shape
    rg_col = 3
    row = lambda w: pl.BlockSpec((tm, w), lambda i: (i, 0))
    return pl.pallas_call(
        _merge_kernel,
        grid=(m // tm,),
        in_specs=[row(d), pl.BlockSpec((tm, SECTION), lambda i: (i, rg_col)), row(SECTION), row(SECTION),
                  pl.BlockSpec(w_bf.shape, lambda i: (0, 0)), pl.BlockSpec((1, d), lambda i: (0, 0))],
        out_specs=row(d),
        out_shape=jax.ShapeDtypeStruct((m, d), F32),
        compiler_params=_cparams(("parallel",)),
        name="mixer_merge",
    )(x, z, ro, do, w_bf, g.reshape(1, d))


def _mem_prompt_kernel(x_ref, gpre_ref, wq_ref, mk_ref, mv_ref, wo_ref, gpost_ref, o_ref):
    x = x_ref[...]
    q = _dot(_rms(x, gpre_ref[...]).astype(BF16), wq_ref[...])
    hd = q.shape[1] // N_MEM_HEADS
    outs = []
    for h in range(N_MEM_HEADS):
        cols = slice(h * hd, (h + 1) * hd)
        s = _dot_nt(q[:, cols].astype(BF16), mk_ref[0, :, cols].astype(BF16)) * (hd ** -0.5)
        e = jnp.exp(s - jnp.max(s, axis=-1, keepdims=True))
        p = e / jnp.sum(e, axis=-1, keepdims=True)
        outs.append(_dot(p.astype(BF16), mv_ref[0, :, cols].astype(BF16)))
    o = jnp.concatenate(outs, axis=-1).astype(BF16)
    o_ref[...] = x + _rms(_dot(o, wo_ref[...]), gpost_ref[...])


def _mem_prompt(x, g_pre, wq_bf, mk, mv, wo_bf, g_post, seq, *, tm):
    m, d = x.shape
    n_mem = mk.shape[1]
    blocks_per_seq = seq // tm
    vec = pl.BlockSpec((1, d), lambda i: (0, 0))
    mat = pl.BlockSpec((d, d), lambda i: (0, 0))
    mem = pl.BlockSpec((1, n_mem, d), lambda i: (i // blocks_per_seq, 0, 0))
    row = pl.BlockSpec((tm, d), lambda i: (i, 0))
    return pl.pallas_call(
        _mem_prompt_kernel,
        grid=(m // tm,),
        in_specs=[row, vec, mat, mem, mem, mat, vec],
        out_specs=row,
        out_shape=jax.ShapeDtypeStruct((m, d), F32),
        compiler_params=_cparams(("parallel",)),
        name="mem_attn_prompt",
    )(x, g_pre.reshape(1, d), wq_bf, mk, mv, wo_bf, g_post.reshape(1, d))


def _mem_sample_kernel(q_ref, k_ref, v_ref, o_ref):
    bb, n_rows, _ = q_ref.shape
    half = n_rows // 2
    key_rows = k_ref.shape[1]
    group = 2 * N_MEM_HEADS
    row = lax.broadcasted_iota(jnp.int32, (half, key_rows), 0)
    col = lax.broadcasted_iota(jnp.int32, (half, key_rows), 1)
    own = (col % group) == (row % N_MEM_HEADS)
    scale = (2 * LANES) ** -0.5
    for b in range(bb):
        sp = _dot_nt(q_ref[b].astype(BF16), k_ref[b].astype(BF16))
        s = (sp[:half] + pltpu.roll(sp[half:], key_rows - N_MEM_HEADS, 1)) * scale
        s = jnp.where(own, s, NEG_BIG)
        e = jnp.exp(s - jnp.max(s, axis=-1, keepdims=True))
        p = e / jnp.sum(e, axis=-1, keepdims=True)
        p2 = jnp.concatenate([p, pltpu.roll(p, N_MEM_HEADS, 1)], axis=0).astype(BF16)
        o = _dot(p2, v_ref[b].astype(BF16))
        o_ref[b] = jnp.concatenate([o[:half], o[half:]], axis=-1)


def _mem_sample(q_rows, k_cache, v_cache, layer, *, bb):
    db, n_rows, _ = q_rows.shape
    key_rows = k_cache.shape[2]
    qspec = pl.BlockSpec((bb, n_rows, LANES), lambda i: (i, 0, 0))
    cspec = pl.BlockSpec((None, bb, key_rows, LANES), lambda i: (layer, i, 0, 0))
    return pl.pallas_call(
        _mem_sample_kernel,
        grid=(db // bb,),
        in_specs=[qspec, cspec, cspec],
        out_specs=pl.BlockSpec((bb, n_rows // 2, 2 * LANES), lambda i: (i, 0, 0)),
        out_shape=jax.ShapeDtypeStruct((db, n_rows // 2, 2 * LANES), F32),
        compiler_params=_cparams(("parallel",)),
        name="mem_attn_sample",
    )(q_rows, k_cache, v_cache)


def _out_proj_kernel(x_ref, a_ref, w_ref, g_ref, o_ref):
    o_ref[...] = x_ref[...] + _rms(_dot(a_ref[...].astype(BF16), w_ref[...]), g_ref[...])


def _out_proj(x, a, w_bf, g, *, tm):
    m, d = x.shape
    row = pl.BlockSpec((tm, d), lambda i: (i, 0))
    return pl.pallas_call(
        _out_proj_kernel,
        grid=(m // tm,),
        in_specs=[row, pl.BlockSpec((tm, a.shape[1]), lambda i: (i, 0)), pl.BlockSpec(w_bf.shape, lambda i: (0, 0)),
                  pl.BlockSpec((1, d), lambda i: (0, 0))],
        out_specs=row,
        out_shape=jax.ShapeDtypeStruct((m, d), F32),
        compiler_params=_cparams(("parallel",)),
        name="out_proj",
    )(x, a, w_bf, g.reshape(1, d))


def _ffn_kernel(x_ref, gpre_ref, wug_ref, wuv_ref, cwg_ref, cwv_ref, cbg_ref, cbv_ref, wd_ref, hg_ref, hv_ref,
                gpost_ref, o_ref, tg_ref, tv_ref, xn_ref, acc_ref, halo_g_ref, halo_v_ref, *, shift,
                blocks_per_seq):
    i = pl.program_id(0)
    j = pl.program_id(1)
    tm = x_ref.shape[0]

    @pl.when(j == 0)
    def _():
        xn_ref[...] = _rms(x_ref[...], gpre_ref[...]).astype(BF16)
        acc_ref[...] = jnp.zeros_like(acc_ref)

    seq_start = (i % blocks_per_seq) == 0

    def conv(w_ref, cw_ref, cb_ref, hist_ref, halo_ref, tail_ref):
        u = _dot(xn_ref[...], w_ref[...])
        prev = jnp.where(seq_start, hist_ref[0], halo_ref[j])
        ue = jnp.concatenate([prev, u], axis=0)
        u2 = ue[:tm]
        u1 = ue[shift:shift + tm]
        tail = u[tm - 2 * shift:]
        halo_ref[j] = tail
        tail_ref[0] = tail
        cw = cw_ref[...]
        return cb_ref[...] + cw[0:1] * u2 + cw[1:2] * u1 + cw[2:3] * u

    gate = conv(wug_ref, cwg_ref, cbg_ref, hg_ref, halo_g_ref, tg_ref)
    val = conv(wuv_ref, cwv_ref, cbv_ref, hv_ref, halo_v_ref, tv_ref)
    acc_ref[...] += _dot((_silu(gate) * val).astype(BF16), wd_ref[...])

    @pl.when(j == pl.num_programs(1) - 1)
    def _():
        o_ref[...] = x_ref[...] + _rms(acc_ref[...], gpost_ref[...])


def _ffn_rows_kernel(x_ref, gpre_ref, wu_ref, cw_ref, cb_ref, wd_ref, hist_ref, gpost_ref, o_ref, tail_ref,
                     xn_ref, h_ref, halo_ref, *, tf, blocks_per_seq):
    i = pl.program_id(0)
    tm = x_ref.shape[0]
    f = wd_ref.shape[0]
    n_tiles = f // tf
    xn_ref[...] = _rms(x_ref[...], gpre_ref[...]).astype(BF16)
    seq_start = (i % blocks_per_seq) == 0
    rows = lax.broadcasted_iota(jnp.int32, (tm, tf), 0)

    def up(col):
        return _dot(xn_ref[...], wu_ref[:, col:col + tf])

    def conv(u, col):
        cols = slice(col, col + tf)
        prev = jnp.where(seq_start, hist_ref[0, :, cols], halo_ref[:, cols])
        u1 = jnp.where(rows == 0, prev[1:2], pltpu.roll(u, 1, 0))
        u2 = jnp.where(rows == 0, prev[0:1], jnp.where(rows == 1, prev[1:2], pltpu.roll(u, 2, 0)))
        tail = u[tm - 2:]
        halo_ref[:, cols] = tail
        tail_ref[0, :, cols] = tail
        cw = cw_ref[:, cols]
        return cb_ref[:, cols] + cw[0:1] * u2 + cw[1:2] * u1 + cw[2:3] * u

    ug, uv = up(0), up(f)
    for t in range(n_tiles):
        cg, cv = ug, uv
        if t + 1 < n_tiles:
            ug, uv = up((t + 1) * tf), up(f + (t + 1) * tf)
        gate = conv(cg, t * tf)
        val = conv(cv, f + t * tf)
        h_ref[:, t * tf:(t + 1) * tf] = (_silu(gate) * val).astype(BF16)
    o_ref[...] = x_ref[...] + _rms(_dot(h_ref[...], wd_ref[...]), gpost_ref[...])


def _conv_ffn_rows(x, g_pre, wu_bf, conv_w, conv_b, wd_bf, hist, g_post, *, tm, tf, blocks_per_seq):
    m, d = x.shape
    f = wd_bf.shape[0]
    nblk = m // tm
    halo = CONV_WIDTH - 1
    const = lambda a: pl.BlockSpec(a.shape, lambda i: (0,) * a.ndim, pipeline_mode=pl.Buffered(1))
    row = pl.BlockSpec((tm, d), lambda i: (i, 0))
    g_pre, g_post, conv_b = g_pre.reshape(1, d), g_post.reshape(1, d), conv_b.reshape(1, 2 * f)
    return pl.pallas_call(
        functools.partial(_ffn_rows_kernel, tf=tf, blocks_per_seq=blocks_per_seq),
        grid=(nblk,),
        in_specs=[row, const(g_pre), const(wu_bf), const(conv_w), const(conv_b), const(wd_bf),
                  pl.BlockSpec((1, halo, 2 * f), lambda i: (i // blocks_per_seq, 0, 0)), const(g_post)],
        out_specs=[row, pl.BlockSpec((1, halo, 2 * f), lambda i: (i, 0, 0))],
        out_shape=[jax.ShapeDtypeStruct((m, d), F32), jax.ShapeDtypeStruct((nblk, halo, 2 * f), F32)],
        scratch_shapes=[pltpu.VMEM((tm, d), BF16), pltpu.VMEM((tm, f), BF16), pltpu.VMEM((halo, 2 * f), F32)],
        compiler_params=_cparams(("arbitrary",)),
        name="conv_ffn_rows",
    )(x, g_pre, wu_bf, conv_w, conv_b, wd_bf, hist, g_post)


def _conv_ffn(x, g_pre, wu_bf, conv_w, conv_b, wd_bf, hist, g_post, *, tm, tf, shift, blocks_per_seq):
    m, d = x.shape
    assert shift % 8 == 0 and tm >= 2 * shift
    f = wd_bf.shape[0]
    nj = f // tf
    nblk = m // tm
    halo = 2 * shift
    up = lambda off: pl.BlockSpec((d, tf), lambda i, j: (0, off + j))
    cw = lambda off: pl.BlockSpec((CONV_WIDTH, tf), lambda i, j: (0, off + j))
    cb = lambda off: pl.BlockSpec((1, tf), lambda i, j: (0, off + j))
    hs = lambda off: pl.BlockSpec((1, halo, tf), lambda i, j: (i // blocks_per_seq, 0, off + j))
    vec = pl.BlockSpec((1, d), lambda i, j: (0, 0))
    row = pl.BlockSpec((tm, d), lambda i, j: (i, 0))
    tail = pl.BlockSpec((1, halo, tf), lambda i, j: (i, 0, j))
    return pl.pallas_call(
        functools.partial(_ffn_kernel, shift=shift, blocks_per_seq=blocks_per_seq),
        grid=(nblk, nj),
        in_specs=[row, vec, up(0), up(nj), cw(0), cw(nj), cb(0), cb(nj),
                  pl.BlockSpec((tf, d), lambda i, j: (j, 0)), hs(0), hs(nj), vec],
        out_specs=[row, tail, tail],
        out_shape=[jax.ShapeDtypeStruct((m, d), F32),
                   jax.ShapeDtypeStruct((nblk, halo, f), F32),
                   jax.ShapeDtypeStruct((nblk, halo, f), F32)],
        scratch_shapes=[pltpu.VMEM((tm, d), BF16), pltpu.VMEM((tm, d), F32),
                        pltpu.VMEM((nj, halo, tf), F32), pltpu.VMEM((nj, halo, tf), F32)],
        compiler_params=_cparams(("arbitrary", "arbitrary")),
        name="conv_ffn",
    )(x, g_pre.reshape(1, d), wu_bf, wu_bf, conv_w, conv_w, conv_b.reshape(1, -1), conv_b.reshape(1, -1), wd_bf,
      hist, hist, g_post.reshape(1, d))


def _rope_tables(pos):
    inv = 1.0 / (ROPE_THETA ** (jnp.arange(0, HEAD_DIM, 2, dtype=F32) / HEAD_DIM))
    ang = pos.astype(F32)[:, None] * inv[None, :]
    ang = jnp.concatenate([ang, ang], axis=-1)
    cos, sin = jnp.cos(ang), jnp.sin(ang)
    first_half = jnp.arange(HEAD_DIM) < HEAD_DIM // 2
    sin_a = jnp.where(first_half[None], -sin, 0.0)
    sin_b = jnp.where(first_half[None], 0.0, sin)
    rep = LANES // HEAD_DIM
    return tuple(jnp.tile(t, (1, rep)) for t in (cos, sin_a, sin_b))


def _tile_for(n, target):
    t = min(n, target)
    assert n % t == 0, (n, t)
    return t


def kernel(x_prompt, x_sample, mem_prompt, state_ret, cache_diff_k, cache_diff_v, page_table, cache_mem_k, cache_mem_v, state_conv, g_mix_pre, g_mix_post, w_in, ret_norm_g, lam_q1, lam_k1, lam_q2, lam_k2, diff_norm_g, w_out, g_mem_pre, g_mem_post, g_mem_kv, w_mq, w_mk, w_mv, w_mo, g_ffn_pre, g_ffn_post, w_up, conv_w, conv_b, w_down):
    B, S, D = x_prompt.shape
    DB, DS, _ = x_sample.shape
    depth = w_in.shape[0]
    n_pool, page = cache_diff_k.shape[1], cache_diff_k.shape[2]
    past = page_table.shape[1] * page
    n_mem = mem_prompt.shape[1]
    two_f = w_up.shape[2]
    mem_hd = D // N_MEM_HEADS
    mp, ms = B * S, DB * DS

    tm_p = _tile_for(S, 512)
    tm_big = _tile_for(S, 1024)
    tile_attn = _tile_for(S, 512)
    tf = 256
    assert (two_f // 2) % tf == 0
    assert mem_hd == 2 * LANES

    tabs_p = _rope_tables(jnp.arange(S, dtype=jnp.int32))
    tabs_s = _rope_tables(jnp.tile(past + jnp.arange(DS, dtype=jnp.int32), DB))
    no_tabs_mem = tuple(jnp.zeros((B * n_mem, LANES), F32) for _ in range(3))
    no_tabs_s = tuple(jnp.zeros((ms, LANES), F32) for _ in range(3))

    kc = cache_diff_k.reshape(depth, n_pool, page * N_DIFF_HEADS, DIFF_DV)
    vc = cache_diff_v.reshape(depth, n_pool, page * N_DIFF_HEADS, DIFF_DV)

    def mem_rows(c):
        c = c.reshape(depth, DB, n_mem, N_MEM_HEADS, 2, LANES).transpose(0, 1, 2, 4, 3, 5)
        return c.reshape(depth, DB, n_mem * 2 * N_MEM_HEADS, LANES)

    mkc = mem_rows(cache_mem_k)
    mvc = mem_rows(cache_mem_v)

    xp = x_prompt.reshape(mp, D)
    xs = x_sample.reshape(ms, D)
    mem2d = mem_prompt.reshape(B * n_mem, D)
    zero_hist = jnp.zeros((B, 2, two_f), F32)
    ret_pad = 8

    outs = {k: [] for k in ("p_ret", "p_dk", "p_dv", "p_mk", "p_mv", "p_conv", "s_ret", "s_dk", "s_dv", "s_conv")}
    rope_secs = (0, 1, 4, 5)
    for l in range(depth):
        lam_init = 0.8 - 0.6 * math.exp(-0.3 * l)
        lam_params = [t[l].reshape(1, HEAD_DIM) for t in (lam_q1, lam_k1, lam_q2, lam_k2)]
        w_in_bf = w_in[l].astype(BF16)
        w_out_bf = w_out[l].astype(BF16)
        w_mq_bf = w_mq[l].astype(BF16)
        w_mo_bf = w_mo[l].astype(BF16)
        w_mkv_bf = jnp.concatenate([w_mk[l], w_mv[l]], axis=1).astype(BF16)
        w_up_bf = w_up[l].astype(BF16)
        w_down_bf = w_down[l].astype(BF16)

        z, dk, dv = _project(xp, g_mix_pre[l], w_in_bf, tabs_p, tm=tm_big, tn=SECTION, rope_sections=rope_secs,
                             k_section=1, n_main=5)
        ro, s_fin = _retention_prompt(z, ret_norm_g[l], B, S, rows_per_step=tm_p)
        do = _diff_prompt(z, dk, dv, lam_params, diff_norm_g[l], lam_init, B, S, tile=tile_attn, heads=4)
        xp = _merge(xp, z, ro, do, w_out_bf, g_mix_post[l], tm=tm_p)
        outs["p_ret"].append(s_fin)
        outs["p_dk"].append(dk.reshape(B, S, N_DIFF_HEADS, DIFF_DV))
        outs["p_dv"].append(dv.reshape(B, S, N_DIFF_HEADS, DIFF_DV))

        zs, dks, dvs = _project(xs, g_mix_pre[l], w_in_bf, tabs_s, tm=ms, tn=SECTION, rope_sections=rope_secs,
                                k_section=1, n_main=5)

        def heads_first(t):
            t = t.reshape(DB, DS, N_RET_HEADS, HEAD_DIM).transpose(0, 2, 1, 3)
            return jnp.pad(t, ((0, 0), (0, 0), (0, ret_pad - DS), (0, 0)))

        ro_s, s_new = _retention_sample(heads_first(zs[:, :SECTION]), heads_first(zs[:, SECTION:2 * SECTION]),
                                        heads_first(zs[:, 2 * SECTION:3 * SECTION]), state_ret[l], ret_norm_g[l],
                                        DS, bb=8)
        ro_s = ro_s[:, :, :DS].transpose(0, 2, 1, 3).reshape(ms, SECTION)

        dq = zs[:, 4 * SECTION:].reshape(DB, DS, N_DIFF_HEADS, 2, HEAD_DIM) * (HEAD_DIM ** -0.5)
        dq = dq.transpose(0, 3, 1, 2, 4)
        zeros = jnp.zeros_like(dq[:, 0])
        qbig = jnp.stack([jnp.concatenate([dq[:, 0], zeros], -1), jnp.concatenate([zeros, dq[:, 1]], -1)], axis=1)
        qbig = qbig.reshape(DB, 2 * DS * N_DIFF_HEADS, DIFF_DV)
        g_rows = jnp.tile(diff_norm_g[l], (DS, 1))
        do_s = _diff_sample(qbig, dks.reshape(DB, DS * N_DIFF_HEADS, DIFF_DV),
                            dvs.reshape(DB, DS * N_DIFF_HEADS, DIFF_DV), kc, vc, page_table, l, lam_params, g_rows,
                            lam_init)
        xs = _merge(xs, zs, ro_s, do_s.reshape(ms, N_DIFF_HEADS * DIFF_DV), w_out_bf, g_mix_post[l], tm=ms)
        outs["s_ret"].append(s_new)
        outs["s_dk"].append(dks.reshape(DB, DS, N_DIFF_HEADS, DIFF_DV))
        outs["s_dv"].append(dvs.reshape(DB, DS, N_DIFF_HEADS, DIFF_DV))

        mkv, = _project(mem2d, g_mem_kv[l], w_mkv_bf, no_tabs_mem, tm=B * n_mem, tn=SECTION)
        mk = mkv[:, :D].reshape(B, n_mem, D)
        mv = mkv[:, D:].reshape(B, n_mem, D)
        xp = _mem_prompt(xp, g_mem_pre[l], w_mq_bf, mk, mv, w_mo_bf, g_mem_post[l], S, tm=tm_p)
        qs, = _project(xs, g_mem_pre[l], w_mq_bf, no_tabs_s, tm=ms, tn=SECTION)
        q_rows = qs.reshape(DB, DS * N_MEM_HEADS, 2, LANES).transpose(0, 2, 1, 3)
        om = _mem_sample(q_rows.reshape(DB, 2 * DS * N_MEM_HEADS, LANES), mkc, mvc, l, bb=4)
        xs = _out_proj(xs, om.reshape(ms, D), w_mo_bf, g_mem_post[l], tm=ms)
        outs["p_mk"].append(mk.reshape(B, n_mem, N_MEM_HEADS, mem_hd))
        outs["p_mv"].append(mv.reshape(B, n_mem, N_MEM_HEADS, mem_hd))

        xp, tails = _conv_ffn_rows(xp, g_ffn_pre[l], w_up_bf, conv_w[l], conv_b[l], w_down_bf, zero_hist,
                                   g_ffn_post[l], tm=tm_p, tf=tf, blocks_per_seq=S // tm_p)
        outs["p_conv"].append(tails[jnp.arange(B) * (S // tm_p) + (S // tm_p - 1)])
        xs_t = xs.reshape(DB, DS, D).transpose(1, 0, 2).reshape(ms, D)
        hist_s = state_conv[l].transpose(1, 0, 2).reshape(1, 2 * DB, two_f)
        xs_t, tg, tv = _conv_ffn(xs_t, g_ffn_pre[l], w_up_bf, conv_w[l], conv_b[l], w_down_bf, hist_s, g_ffn_post[l],
                                 tm=ms, tf=tf, shift=DB, blocks_per_seq=1)
        xs = xs_t.reshape(DS, DB, D).transpose(1, 0, 2).reshape(ms, D)
        outs["s_conv"].append(jnp.concatenate([tg[0], tv[0]], axis=-1).reshape(2, DB, two_f).transpose(1, 0, 2))

    st = lambda k: jnp.stack(outs[k])
    return (xp.reshape(B, S, D), xs.reshape(DB, DS, D), st("p_ret"), st("p_dk"), st("p_dv"), st("p_mk"), st("p_mv"),
            st("p_conv"), st("s_ret"), st("s_dk"), st("s_dv"), st("s_conv"))
```

```python
import functools
import math

import jax
import jax.numpy as jnp
from jax import lax
from jax.experimental import pallas as pl
from jax.experimental.pallas import tpu as pltpu

F32 = jnp.float32
BF16 = jnp.bfloat16

HEAD_DIM = 64
N_RET_HEADS = 8
N_DIFF_HEADS = 4
DIFF_DV = 2 * HEAD_DIM
SECTION = 512
N_SECTIONS = 7
N_MEM_HEADS = 4
CONV_WIDTH = 3
ROPE_THETA = 10000.0
EPS = 1e-6
NEG_BIG = -1e30
LOG2_E = math.log2(math.e)
RET_CHUNK = 128

LANES = 128
BF16_SUBLANES = 16
ONES_ROWS = BF16_SUBLANES
V7X_VMEM_BYTES = 64 * 1024 * 1024
VMEM_LIMIT = (V7X_VMEM_BYTES * 3) // 4


def _cparams(semantics):
    return pltpu.CompilerParams(dimension_semantics=semantics, vmem_limit_bytes=VMEM_LIMIT)


def _layer_spec(w, layer):
    zeros = (0,) * (w.ndim - 1)
    return pl.BlockSpec((None,) + w.shape[1:], lambda *_: (layer,) + zeros, pipeline_mode=pl.Buffered(1))


def _rms(x, g):
    return x * lax.rsqrt(jnp.mean(x * x, axis=-1, keepdims=True) + EPS) * g


def _dot(a, b):
    return jnp.dot(a, b, preferred_element_type=F32)


def _dot_nt(a, b):
    return lax.dot_general(a, b, (((1,), (1,)), ((), ())), preferred_element_type=F32)


def _silu(x):
    return x * (1.0 / (1.0 + jnp.exp(-x)))


def _proj_kernel(x_ref, g_ref, w_ref, cos_ref, sa_ref, sb_ref, *out_and_scratch, rope_sections, k_section,
                 n_main, tn):
    outs, xn_ref = out_and_scratch[:-1], out_and_scratch[-1]
    xn_ref[...] = _rms(x_ref[...], g_ref[...]).astype(BF16)
    n_sections = w_ref.shape[1] // tn

    def matmul(sec):
        return _dot(xn_ref[...], w_ref[:, sec * tn:(sec + 1) * tn])

    def emit(sec, acc):
        o_ref, col0 = (outs[0], sec * tn) if sec < n_main else (outs[sec - n_main + 1], 0)
        if sec in rope_sections:
            scale = HEAD_DIM ** -0.5 if sec == k_section else 1.0
            c, sa, sb = cos_ref[...], sa_ref[...], sb_ref[...]
            for t in range(tn // LANES):
                a = acc[:, t * LANES:(t + 1) * LANES]
                r = a * c + pltpu.roll(a, LANES - HEAD_DIM // 2, 1) * sa + pltpu.roll(a, HEAD_DIM // 2, 1) * sb
                o_ref[:, col0 + t * LANES:col0 + (t + 1) * LANES] = r * scale if scale != 1.0 else r
        else:
            o_ref[:, col0:col0 + tn] = acc

    acc_next = matmul(0)
    for sec in range(n_sections):
        acc = acc_next
        if sec + 1 < n_sections:
            acc_next = matmul(sec + 1)
        emit(sec, acc)


def _project(x, g, w_bf, layer, tabs, *, tm, tn, rope_sections=(), k_section=-1, n_main=None):
    m, d = x.shape
    n = w_bf.shape[2]
    n_sec = n // tn
    n_main = n_sec if n_main is None else n_main
    cos, sa, sb = tabs
    tab_blocks = cos.shape[0] // tm
    tab_spec = pl.BlockSpec((tm, LANES), lambda i: (i % tab_blocks, 0))
    out_shape = [jax.ShapeDtypeStruct((m, n_main * tn), F32)]
    out_specs = [pl.BlockSpec((tm, n_main * tn), lambda i: (i, 0))]
    for _ in range(n_sec - n_main):
        out_shape.append(jax.ShapeDtypeStruct((m, tn), F32))
        out_specs.append(pl.BlockSpec((tm, tn), lambda i: (i, 0)))
    kern = functools.partial(_proj_kernel, rope_sections=tuple(rope_sections), k_section=k_section, n_main=n_main,
                             tn=tn)
    return pl.pallas_call(
        kern,
        grid=(m // tm,),
        in_specs=[pl.BlockSpec((tm, d), lambda i: (i, 0)),
                  pl.BlockSpec((1, d), lambda i: (0, 0)),
                  _layer_spec(w_bf, layer),
                  tab_spec, tab_spec, tab_spec],
        out_specs=out_specs,
        out_shape=out_shape,
        scratch_shapes=[pltpu.VMEM((tm, d), BF16)],
        compiler_params=_cparams(("parallel",)),
        name="project",
    )(x, g.reshape(1, d), w_bf, cos, sa, sb)


def _ret_prompt_kernel(q_ref, k_ref, v_ref, dec_ref, rw_ref, ww_ref, cd_ref, g_ref, o_ref, sfin_ref, s_ref, *,
                       n_chunks):
    r = pl.program_id(1)

    @pl.when(r == 0)
    def _():
        s_ref[...] = jnp.zeros_like(s_ref)

    lane = lax.broadcasted_iota(jnp.int32, (RET_CHUNK, LANES), 1)
    first = lane < HEAD_DIM
    srow = lax.broadcasted_iota(jnp.int32, (LANES, LANES), 0)
    scol = lax.broadcasted_iota(jnp.int32, (LANES, LANES), 1)
    same_head = (srow < HEAD_DIM) == (scol < HEAD_DIM)

    def half_mean(t):
        tot = jnp.sum(t, axis=-1, keepdims=True)
        lo = jnp.sum(jnp.where(first, t, 0.0), axis=-1, keepdims=True)
        return jnp.where(first, lo, tot - lo) * (1.0 / HEAD_DIM)

    def chunk(c, carry):
        row = pl.multiple_of(c * RET_CHUNK, RET_CHUNK)
        rows = pl.ds(row, RET_CHUNK)
        for p in range(N_RET_HEADS // 2):
            cols = slice(p * LANES, (p + 1) * LANES)
            q = q_ref[rows, cols]
            k = k_ref[rows, cols]
            vb = v_ref[rows, cols].astype(BF16)
            kb = k.astype(BF16)
            s = s_ref[p]
            sc_a = (_dot_nt(jnp.where(first, q, 0.0).astype(BF16), kb) * dec_ref[2 * p]).astype(BF16)
            sc_b = (_dot_nt(jnp.where(first, 0.0, q).astype(BF16), kb) * dec_ref[2 * p + 1]).astype(BF16)
            o = jnp.where(first, _dot(sc_a, vb), _dot(sc_b, vb)) + _dot(q.astype(BF16), s.astype(BF16)) * rw_ref[p]
            kw_t = (k * ww_ref[p]).T.astype(BF16)
            s_ref[p] = s * cd_ref[p] + jnp.where(same_head, _dot(kw_t, vb), 0.0)
            d = o - half_mean(o)
            o_ref[rows, cols] = d * lax.rsqrt(half_mean(d * d) + EPS) * g_ref[p]
        return carry

    lax.fori_loop(0, n_chunks, chunk, 0)

    @pl.when(r == pl.num_programs(1) - 1)
    def _():
        for h in range(N_RET_HEADS):
            lo = (h % 2) * HEAD_DIM
            sfin_ref[0, h] = s_ref[h // 2][lo:lo + HEAD_DIM, lo:lo + HEAD_DIM]


def _retention_tables(length):
    log_g = jnp.log1p(-jnp.exp2(-5.0 - jnp.arange(N_RET_HEADS, dtype=F32)))
    n = jnp.arange(length, dtype=F32)
    diff = n[:, None] - n[None, :]
    dec = jnp.where((diff >= 0)[None], jnp.exp(log_g[:, None, None] * jnp.maximum(diff, 0.0)[None]), 0.0)
    read_w = jnp.exp(log_g[:, None] * (n + 1.0)[None])
    write_w = jnp.exp(log_g[:, None] * (length - 1.0 - n)[None])
    chunk_decay = jnp.exp(log_g * length)
    return dec, read_w, write_w, chunk_decay


def _retention_prompt(z, ret_g, batch, seq, *, rows_per_step):
    m = z.shape[0]
    nr = seq // rows_per_step
    dec, read_w, write_w, chunk_decay = _retention_tables(RET_CHUNK)
    n_pairs = N_RET_HEADS // 2

    def per_lane(t):
        t = jnp.broadcast_to(t[:, :, None], (N_RET_HEADS, t.shape[1], HEAD_DIM))
        return t.reshape(n_pairs, 2, t.shape[1], HEAD_DIM).transpose(0, 2, 1, 3).reshape(n_pairs, t.shape[1], LANES)

    rw = per_lane(read_w)
    ww = per_lane(write_w)
    cd = jnp.repeat(per_lane(jnp.broadcast_to(chunk_decay[:, None], (N_RET_HEADS, HEAD_DIM))), 2, axis=1)
    full3 = lambda shape: pl.BlockSpec(shape, lambda b, r: (0, 0, 0))
    sec = lambda c: pl.BlockSpec((rows_per_step, SECTION), lambda b, r: (b * nr + r, c))
    return pl.pallas_call(
        functools.partial(_ret_prompt_kernel, n_chunks=rows_per_step // RET_CHUNK),
        grid=(batch, nr),
        in_specs=[sec(0), sec(1), sec(2),
                  full3((N_RET_HEADS, RET_CHUNK, RET_CHUNK)),
                  full3((n_pairs, RET_CHUNK, LANES)),
                  full3((n_pairs, RET_CHUNK, LANES)),
                  full3((n_pairs, LANES, LANES)),
                  full3((n_pairs, 1, LANES))],
        out_specs=[pl.BlockSpec((rows_per_step, SECTION), lambda b, r: (b * nr + r, 0)),
                   pl.BlockSpec((1, N_RET_HEADS, HEAD_DIM, HEAD_DIM), lambda b, r: (b, 0, 0, 0))],
        out_shape=[jax.ShapeDtypeStruct((m, SECTION), F32),
                   jax.ShapeDtypeStruct((batch, N_RET_HEADS, HEAD_DIM, HEAD_DIM), F32)],
        scratch_shapes=[pltpu.VMEM((n_pairs, LANES, LANES), F32)],
        compiler_params=_cparams(("parallel", "arbitrary")),
        name="retention_prompt",
    )(z, z, z, dec, rw, ww, cd, ret_g.reshape(n_pairs, 1, LANES))


def _ret_sample_kernel(q_ref, k_ref, v_ref, s_ref, dec_ref, rw_ref, ww_ref, cd_ref, g_ref, o_ref, snew_ref):
    n = q_ref.shape[0] * q_ref.shape[1]
    length = q_ref.shape[2]
    q = q_ref[...].reshape(n, length, HEAD_DIM)
    k = k_ref[...].reshape(n, length, HEAD_DIM)
    vb = v_ref[...].reshape(n, length, HEAD_DIM).astype(BF16)
    s = s_ref[...].reshape(n, HEAD_DIM, HEAD_DIM)
    qb = q.astype(BF16)
    scores = jnp.einsum('nqd,nkd->nqk', qb, k.astype(BF16), preferred_element_type=F32) * dec_ref[...]
    o = jnp.einsum('nqk,nke->nqe', scores.astype(BF16), vb, preferred_element_type=F32)
    o = o + jnp.einsum('nqd,nde->nqe', qb, s.astype(BF16), preferred_element_type=F32) * rw_ref[...]
    kw = (k * ww_ref[...]).astype(BF16)
    s_new = s * cd_ref[...] + jnp.einsum('nld,nle->nde', kw, vb, preferred_element_type=F32)
    mu = jnp.mean(o, axis=-1, keepdims=True)
    var = jnp.mean(jnp.square(o - mu), axis=-1, keepdims=True)
    o_ref[...] = ((o - mu) * lax.rsqrt(var + EPS) * g_ref[...]).reshape(o_ref.shape)
    snew_ref[...] = s_new.reshape(snew_ref.shape)


def _retention_sample(q, k, v, state, ret_g, n_tokens, *, bb):
    db, nh, length, _ = q.shape
    dec, read_w, write_w, chunk_decay = _retention_tables(n_tokens)
    pad = length - n_tokens
    dec = jnp.pad(dec, ((0, 0), (0, pad), (0, pad)))
    read_w = jnp.pad(read_w, ((0, 0), (0, pad)))
    write_w = jnp.pad(write_w, ((0, 0), (0, pad)))
    n = bb * nh
    tile = lambda t: jnp.tile(t, (bb,) + (1,) * (t.ndim - 1))
    dec_t = tile(dec)
    rw_t = tile(jnp.broadcast_to(read_w[:, :, None], (nh, length, HEAD_DIM)))
    ww_t = tile(jnp.broadcast_to(write_w[:, :, None], (nh, length, HEAD_DIM)))
    cd_t = tile(jnp.broadcast_to(chunk_decay[:, None, None], (nh, HEAD_DIM, HEAD_DIM)))
    g_t = tile(jnp.broadcast_to(ret_g[:, None, :], (nh, length, HEAD_DIM)))
    blk = pl.BlockSpec((bb, nh, length, HEAD_DIM), lambda i: (i, 0, 0, 0))
    sblk = pl.BlockSpec((bb, nh, HEAD_DIM, HEAD_DIM), lambda i: (i, 0, 0, 0))
    full = lambda a: pl.BlockSpec(a.shape, lambda i: (0,) * a.ndim)
    return pl.pallas_call(
        _ret_sample_kernel,
        grid=(db // bb,),
        in_specs=[blk, blk, blk, sblk, full(dec_t), full(rw_t), full(ww_t), full(cd_t), full(g_t)],
        out_specs=[blk, sblk],
        out_shape=[jax.ShapeDtypeStruct(q.shape, F32), jax.ShapeDtypeStruct(state.shape, F32)],
        compiler_params=_cparams(("parallel",)),
        name="retention_sample",
    )(q, k, v, state, dec_t, rw_t, ww_t, cd_t, g_t)


def _lambda(lq1_ref, lk1_ref, lq2_ref, lk2_ref, lam_init):
    e1 = jnp.exp(jnp.sum(lq1_ref[...] * lk1_ref[...], axis=-1, keepdims=True))
    e2 = jnp.exp(jnp.sum(lq2_ref[...] * lk2_ref[...], axis=-1, keepdims=True))
    return e1 - e2 + lam_init


def _diff_prompt_kernel(qi_ref, ki_ref, q_ref, k_ref, v_ref, lq1_ref, lk1_ref, lq2_ref, lk2_ref, g_ref, o_ref,
                        m_ref, acc_ref, qm_ref, *, lam_init, heads):
    p = pl.program_id(2)
    qi = qi_ref[p]
    ki = ki_ref[p]
    tk = k_ref.shape[0]

    @pl.when(ki == 0)
    def _():
        m_ref[...] = jnp.full_like(m_ref, NEG_BIG)
        acc_ref[...] = jnp.zeros_like(acc_ref)
        lane = lax.broadcasted_iota(jnp.int32, (q_ref.shape[0], DIFF_DV), 1)
        for h in range(heads):
            q = q_ref[:, h * DIFF_DV:(h + 1) * DIFF_DV] * (HEAD_DIM ** -0.5 * LOG2_E)
            qm_ref[2 * h] = jnp.where(lane < HEAD_DIM, q, 0.0).astype(BF16)
            qm_ref[2 * h + 1] = jnp.where(lane < HEAD_DIM, 0.0, q).astype(BF16)

    def step(masked):
        ones = jnp.ones((ONES_ROWS, tk), BF16)
        kb = [None] * heads

        def scores(idx):
            h = idx // 2
            if kb[h] is None:
                kb[h] = k_ref[:, h * DIFF_DV:(h + 1) * DIFF_DV].astype(BF16)
            return _dot_nt(kb[h], qm_ref[idx])

        def accumulate(idx, s, v_ext):
            if masked:
                key = lax.broadcasted_iota(jnp.int32, s.shape, 0)
                qry = lax.broadcasted_iota(jnp.int32, s.shape, 1)
                s = jnp.where(key <= qry, s, NEG_BIG)
            m_old = m_ref[idx]
            m_new = jnp.maximum(m_old, jnp.max(s, axis=0, keepdims=True))
            alpha = jnp.exp2(m_old - m_new)
            e = jnp.exp2(s - m_new).astype(BF16)
            acc_ref[idx] = alpha * acc_ref[idx] + _dot(v_ext, e)
            m_ref[idx] = m_new

        n = 2 * heads
        s_next = scores(0)
        v_ext = None
        for idx in range(n):
            s_cur = s_next
            if idx + 1 < n:
                s_next = scores(idx + 1)
            if idx % 2 == 0:
                hcols = slice((idx // 2) * DIFF_DV, (idx // 2 + 1) * DIFF_DV)
                v_ext = jnp.concatenate([v_ref[:, hcols].T.astype(BF16), ones], axis=0)
            accumulate(idx, s_cur, v_ext)

    @pl.when(ki < qi)
    def _():
        step(False)

    @pl.when(ki == qi)
    def _():
        step(True)
        lam = _lambda(lq1_ref, lk1_ref, lq2_ref, lk2_ref, lam_init)
        for h in range(heads):
            a1 = acc_ref[2 * h]
            a2 = acc_ref[2 * h + 1]
            o_t = (a1[:DIFF_DV] / a1[DIFF_DV:DIFF_DV + 1] - lam * (a2[:DIFF_DV] / a2[DIFF_DV:DIFF_DV + 1]))
            o_ref[:, h * DIFF_DV:(h + 1) * DIFF_DV] = _rms(o_t.T, g_ref[h]) * (1.0 - lam_init)


def _diff_prompt(z, dk, dv, lam_params, diff_g, lam_init, batch, seq, *, tile, heads):
    m = z.shape[0]
    nt = seq // tile
    width = heads * DIFF_DV
    groups = N_DIFF_HEADS // heads
    pairs = [(qi, ki) for qi in range(nt) for ki in range(qi + 1)]
    qi_tab = jnp.asarray([p[0] for p in pairs], jnp.int32)
    ki_tab = jnp.asarray([p[1] for p in pairs], jnp.int32)
    q_col0 = 4 * SECTION // width
    lam_spec = pl.BlockSpec((1, HEAD_DIM), lambda b, h, p, qi, ki: (0, 0))
    grid_spec = pltpu.PrefetchScalarGridSpec(
        num_scalar_prefetch=2,
        grid=(batch, groups, len(pairs)),
        in_specs=[pl.BlockSpec((tile, width), lambda b, h, p, qi, ki: (b * nt + qi[p], q_col0 + h)),
                  pl.BlockSpec((tile, width), lambda b, h, p, qi, ki: (b * nt + ki[p], h)),
                  pl.BlockSpec((tile, width), lambda b, h, p, qi, ki: (b * nt + ki[p], h)),
                  lam_spec, lam_spec, lam_spec, lam_spec,
                  pl.BlockSpec((heads, 1, DIFF_DV), lambda b, h, p, qi, ki: (h, 0, 0))],
        out_specs=pl.BlockSpec((tile, width), lambda b, h, p, qi, ki: (b * nt + qi[p], h)),
        scratch_shapes=[pltpu.VMEM((2 * heads, 1, tile), F32),
                        pltpu.VMEM((2 * heads, DIFF_DV + ONES_ROWS, tile), F32),
                        pltpu.VMEM((2 * heads, tile, DIFF_DV), BF16)],
    )
    return pl.pallas_call(
        functools.partial(_diff_prompt_kernel, lam_init=lam_init, heads=heads),
        grid_spec=grid_spec,
        out_shape=jax.ShapeDtypeStruct((m, N_DIFF_HEADS * DIFF_DV), F32),
        compiler_params=_cparams(("parallel", "parallel", "arbitrary")),
        name="diff_attn_prompt",
    )(qi_tab, ki_tab, z, dk, dv, *lam_params, diff_g.reshape(N_DIFF_HEADS, 1, DIFF_DV))


def _diff_sample_kernel(pt_ref, q_ref, kn_ref, vn_ref, *rest, n_pages, lam_init):
    k_pages = rest[:n_pages]
    v_pages = rest[n_pages:2 * n_pages]
    g_ref, lq1_ref, lk1_ref, lq2_ref, lk2_ref, o_ref, s_ref = rest[2 * n_pages:]
    half = q_ref.shape[1]
    n_rows = 2 * half
    page_rows = k_pages[0].shape[0]
    qf = q_ref[0] * (HEAD_DIM ** -0.5)
    lane = lax.broadcasted_iota(jnp.int32, qf.shape, 1)
    q = jnp.concatenate([jnp.where(lane < HEAD_DIM, qf, 0.0), jnp.where(lane < HEAD_DIM, 0.0, qf)],
                        axis=0).astype(BF16)

    row_h = lax.broadcasted_iota(jnp.int32, (n_rows, page_rows), 0) % N_DIFF_HEADS
    col_h = lax.broadcasted_iota(jnp.int32, (n_rows, page_rows), 1) % N_DIFF_HEADS
    same_head = row_h == col_h
    m = jnp.full((n_rows, 1), NEG_BIG, F32)
    for p in range(n_pages):
        s = jnp.where(same_head, _dot_nt(q, k_pages[p][...].astype(BF16)), NEG_BIG)
        s_ref[:, p * page_rows:(p + 1) * page_rows] = s
        m = jnp.maximum(m, jnp.max(s, axis=-1, keepdims=True))

    n_new = kn_ref.shape[1]
    r = lax.broadcasted_iota(jnp.int32, (n_rows, n_new), 0)
    c = lax.broadcasted_iota(jnp.int32, (n_rows, n_new), 1)
    ok = (r % N_DIFF_HEADS == c % N_DIFF_HEADS) & (c // N_DIFF_HEADS <= (r % half) // N_DIFF_HEADS)
    s_new = jnp.where(ok, _dot_nt(q, kn_ref[0].astype(BF16)), NEG_BIG)
    m = jnp.maximum(m, jnp.max(s_new, axis=-1, keepdims=True))

    e_new = jnp.exp(s_new - m)
    l = jnp.sum(e_new, axis=-1, keepdims=True)
    for p in range(n_pages):
        cols = slice(p * page_rows, (p + 1) * page_rows)
        e = jnp.exp(s_ref[:, cols] - m)
        s_ref[:, cols] = e
        l = l + jnp.sum(e, axis=-1, keepdims=True)

    lam = _lambda(lq1_ref, lk1_ref, lq2_ref, lk2_ref, lam_init)
    inv = 1.0 / l
    w1 = inv[:half]
    w2 = lam * inv[half:]
    a_new = (e_new[:half] * w1 - e_new[half:] * w2).astype(BF16)
    o = _dot(a_new, vn_ref[0].astype(BF16))
    for p in range(n_pages):
        cols = slice(p * page_rows, (p + 1) * page_rows)
        a = (s_ref[:half, cols] * w1 - s_ref[half:, cols] * w2).astype(BF16)
        o = o + _dot(a, v_pages[p][...].astype(BF16))
    o_ref[0] = _rms(o, g_ref[...]) * (1.0 - lam_init)


def _diff_sample(q_new, k_new, v_new, k_cache, v_cache, page_table, layer, lam_params, g_rows, lam_init):
    db = q_new.shape[0]
    n_rows = 2 * q_new.shape[1]
    n_pages = page_table.shape[1]
    page_rows = k_cache.shape[2]
    n_new = k_new.shape[1]

    def page_spec(p):
        return pl.BlockSpec((None, None, page_rows, LANES), lambda b, pt: (layer, pt[b * n_pages + p], 0, 0))

    small = lambda rows: pl.BlockSpec((1, rows, LANES), lambda b, pt: (b, 0, 0))
    lam_spec = pl.BlockSpec((1, HEAD_DIM), lambda b, pt: (0, 0))
    grid_spec = pltpu.PrefetchScalarGridSpec(
        num_scalar_prefetch=1,
        grid=(db,),
        in_specs=[small(n_new), small(n_new), small(n_new)]
        + [page_spec(p) for p in range(n_pages)] * 2
        + [pl.BlockSpec((n_rows // 2, LANES), lambda b, pt: (0, 0)), lam_spec, lam_spec, lam_spec, lam_spec],
        out_specs=small(n_rows // 2),
        scratch_shapes=[pltpu.VMEM((n_rows, n_pages * page_rows), F32)],
    )
    return pl.pallas_call(
        functools.partial(_diff_sample_kernel, n_pages=n_pages, lam_init=lam_init),
        grid_spec=grid_spec,
        out_shape=jax.ShapeDtypeStruct((db, n_rows // 2, LANES), F32),
        compiler_params=_cparams(("parallel",)),
        name="diff_attn_sample",
    )(page_table.reshape(-1), q_new, k_new, v_new, *([k_cache] * n_pages), *([v_cache] * n_pages), g_rows,
      *lam_params)


def _merge_kernel(x_ref, rg_ref, ro_ref, do_ref, w_ref, g_ref, o_ref):
    a = jnp.concatenate([_silu(rg_ref[...]) * ro_ref[...], do_ref[...]], axis=-1).astype(BF16)
    o_ref[...] = x_ref[...] + _rms(_dot(a, w_ref[...]), g_ref[...])


def _merge(x, z, ro, do, w_bf, layer, g, *, tm):
    m, d = x.shape
    rg_col = 3
    row = lambda w: pl.BlockSpec((tm, w), lambda i: (i, 0))
    return pl.pallas_call(
        _merge_kernel,
        grid=(m // tm,),
        in_specs=[row(d), pl.BlockSpec((tm, SECTION), lambda i: (i, rg_col)), row(SECTION), row(SECTION),
                  _layer_spec(w_bf, layer), pl.BlockSpec((1, d), lambda i: (0, 0))],
        out_specs=row(d),
        out_shape=jax.ShapeDtypeStruct((m, d), F32),
        compiler_params=_cparams(("parallel",)),
        name="mixer_merge",
    )(x, z, ro, do, w_bf, g.reshape(1, d))


def _mem_prompt_kernel(x_ref, gpre_ref, wq_ref, mk_ref, mv_ref, wo_ref, gpost_ref, o_ref):
    x = x_ref[...]
    q = _dot(_rms(x, gpre_ref[...]).astype(BF16), wq_ref[...])
    hd = q.shape[1] // N_MEM_HEADS
    outs = []
    for h in range(N_MEM_HEADS):
        cols = slice(h * hd, (h + 1) * hd)
        s = _dot_nt(q[:, cols].astype(BF16), mk_ref[0, :, cols].astype(BF16)) * (hd ** -0.5)
        e = jnp.exp(s - jnp.max(s, axis=-1, keepdims=True))
        p = e / jnp.sum(e, axis=-1, keepdims=True)
        outs.append(_dot(p.astype(BF16), mv_ref[0, :, cols].astype(BF16)))
    o = jnp.concatenate(outs, axis=-1).astype(BF16)
    o_ref[...] = x + _rms(_dot(o, wo_ref[...]), gpost_ref[...])


def _mem_prompt(x, g_pre, wq_bf, mk, mv, wo_bf, layer, g_post, seq, *, tm):
    m, d = x.shape
    n_mem = mk.shape[1]
    blocks_per_seq = seq // tm
    vec = pl.BlockSpec((1, d), lambda i: (0, 0))
    mem = pl.BlockSpec((1, n_mem, d), lambda i: (i // blocks_per_seq, 0, 0))
    row = pl.BlockSpec((tm, d), lambda i: (i, 0))
    return pl.pallas_call(
        _mem_prompt_kernel,
        grid=(m // tm,),
        in_specs=[row, vec, _layer_spec(wq_bf, layer), mem, mem, _layer_spec(wo_bf, layer), vec],
        out_specs=row,
        out_shape=jax.ShapeDtypeStruct((m, d), F32),
        compiler_params=_cparams(("parallel",)),
        name="mem_attn_prompt",
    )(x, g_pre.reshape(1, d), wq_bf, mk, mv, wo_bf, g_post.reshape(1, d))


def _mem_sample_kernel(q_ref, k_ref, v_ref, o_ref):
    bb, n_rows, _ = q_ref.shape
    half = n_rows // 2
    key_rows = k_ref.shape[1]
    group = 2 * N_MEM_HEADS
    row = lax.broadcasted_iota(jnp.int32, (half, key_rows), 0)
    col = lax.broadcasted_iota(jnp.int32, (half, key_rows), 1)
    own = (col % group) == (row % N_MEM_HEADS)
    scale = (2 * LANES) ** -0.5
    for b in range(bb):
        sp = _dot_nt(q_ref[b].astype(BF16), k_ref[b].astype(BF16))
        s = (sp[:half] + pltpu.roll(sp[half:], key_rows - N_MEM_HEADS, 1)) * scale
        s = jnp.where(own, s, NEG_BIG)
        e = jnp.exp(s - jnp.max(s, axis=-1, keepdims=True))
        p = e / jnp.sum(e, axis=-1, keepdims=True)
        p2 = jnp.concatenate([p, pltpu.roll(p, N_MEM_HEADS, 1)], axis=0).astype(BF16)
        o = _dot(p2, v_ref[b].astype(BF16))
        o_ref[b] = jnp.concatenate([o[:half], o[half:]], axis=-1)


def _mem_sample(q_rows, k_cache, v_cache, layer, *, bb):
    db, n_rows, _ = q_rows.shape
    key_rows = k_cache.shape[2]
    qspec = pl.BlockSpec((bb, n_rows, LANES), lambda i: (i, 0, 0))
    cspec = pl.BlockSpec((None, bb, key_rows, LANES), lambda i: (layer, i, 0, 0))
    return pl.pallas_call(
        _mem_sample_kernel,
        grid=(db // bb,),
        in_specs=[qspec, cspec, cspec],
        out_specs=pl.BlockSpec((bb, n_rows // 2, 2 * LANES), lambda i: (i, 0, 0)),
        out_shape=jax.ShapeDtypeStruct((db, n_rows // 2, 2 * LANES), F32),
        compiler_params=_cparams(("parallel",)),
        name="mem_attn_sample",
    )(q_rows, k_cache, v_cache)


def _out_proj_kernel(x_ref, a_ref, w_ref, g_ref, o_ref):
    o_ref[...] = x_ref[...] + _rms(_dot(a_ref[...].astype(BF16), w_ref[...]), g_ref[...])


def _out_proj(x, a, w_bf, layer, g, *, tm):
    m, d = x.shape
    row = pl.BlockSpec((tm, d), lambda i: (i, 0))
    return pl.pallas_call(
        _out_proj_kernel,
        grid=(m // tm,),
        in_specs=[row, pl.BlockSpec((tm, a.shape[1]), lambda i: (i, 0)), _layer_spec(w_bf, layer),
                  pl.BlockSpec((1, d), lambda i: (0, 0))],
        out_specs=row,
        out_shape=jax.ShapeDtypeStruct((m, d), F32),
        compiler_params=_cparams(("parallel",)),
        name="out_proj",
    )(x, a, w_bf, g.reshape(1, d))


def _ffn_kernel(x_ref, gpre_ref, wug_ref, wuv_ref, cwg_ref, cwv_ref, cbg_ref, cbv_ref, wd_ref, hg_ref, hv_ref,
                gpost_ref, o_ref, tg_ref, tv_ref, xn_ref, acc_ref, halo_g_ref, halo_v_ref, *, shift,
                blocks_per_seq):
    i = pl.program_id(0)
    j = pl.program_id(1)
    tm = x_ref.shape[0]

    @pl.when(j == 0)
    def _():
        xn_ref[...] = _rms(x_ref[...], gpre_ref[...]).astype(BF16)
        acc_ref[...] = jnp.zeros_like(acc_ref)

    seq_start = (i % blocks_per_seq) == 0

    def conv(w_ref, cw_ref, cb_ref, hist_ref, halo_ref, tail_ref):
        u = _dot(xn_ref[...], w_ref[...])
        prev = jnp.where(seq_start, hist_ref[0], halo_ref[j])
        ue = jnp.concatenate([prev, u], axis=0)
        u2 = ue[:tm]
        u1 = ue[shift:shift + tm]
        tail = u[tm - 2 * shift:]
        halo_ref[j] = tail
        tail_ref[0] = tail
        cw = cw_ref[...]
        return cb_ref[...] + cw[0:1] * u2 + cw[1:2] * u1 + cw[2:3] * u

    gate = conv(wug_ref, cwg_ref, cbg_ref, hg_ref, halo_g_ref, tg_ref)
    val = conv(wuv_ref, cwv_ref, cbv_ref, hv_ref, halo_v_ref, tv_ref)
    acc_ref[...] += _dot((_silu(gate) * val).astype(BF16), wd_ref[...])

    @pl.when(j == pl.num_programs(1) - 1)
    def _():
        o_ref[...] = x_ref[...] + _rms(acc_ref[...], gpost_ref[...])


def _ffn_rows_kernel(x_ref, gpre_ref, wu_ref, cw_ref, cb_ref, wd_ref, hist_ref, gpost_ref, o_ref, tail_ref,
                     xn_ref, h_ref, halo_ref, *, tf, blocks_per_seq):
    i = pl.program_id(0)
    tm = x_ref.shape[0]
    f = wd_ref.shape[0]
    n_tiles = f // tf
    xn_ref[...] = _rms(x_ref[...], gpre_ref[...]).astype(BF16)
    seq_start = (i % blocks_per_seq) == 0
    rows = lax.broadcasted_iota(jnp.int32, (tm, tf), 0)

    def up(col):
        return _dot(xn_ref[...], wu_ref[:, col:col + tf])

    def conv(u, col):
        cols = slice(col, col + tf)
        prev = jnp.where(seq_start, hist_ref[0, :, cols], halo_ref[:, cols])
        u1 = jnp.where(rows == 0, prev[1:2], pltpu.roll(u, 1, 0))
        u2 = jnp.where(rows == 0, prev[0:1], jnp.where(rows == 1, prev[1:2], pltpu.roll(u, 2, 0)))
        tail = u[tm - 2:]
        halo_ref[:, cols] = tail
        tail_ref[0, :, cols] = tail
        cw = cw_ref[:, cols]
        return cb_ref[:, cols] + cw[0:1] * u2 + cw[1:2] * u1 + cw[2:3] * u

    ug, uv = up(0), up(f)
    for t in range(n_tiles):
        cg, cv = ug, uv
        if t + 1 < n_tiles:
            ug, uv = up((t + 1) * tf), up(f + (t + 1) * tf)
        gate = conv(cg, t * tf)
        val = conv(cv, f + t * tf)
        h_ref[:, t * tf:(t + 1) * tf] = (_silu(gate) * val).astype(BF16)
    o_ref[...] = x_ref[...] + _rms(_dot(h_ref[...], wd_ref[...]), gpost_ref[...])


def _conv_ffn_rows(x, g_pre, wu_bf, conv_w, conv_b, wd_bf, layer, hist, g_post, *, tm, tf, blocks_per_seq):
    m, d = x.shape
    f = wd_bf.shape[1]
    nblk = m // tm
    halo = CONV_WIDTH - 1
    const = lambda a: pl.BlockSpec(a.shape, lambda i: (0,) * a.ndim, pipeline_mode=pl.Buffered(1))
    row = pl.BlockSpec((tm, d), lambda i: (i, 0))
    g_pre, g_post, conv_b = g_pre.reshape(1, d), g_post.reshape(1, d), conv_b.reshape(1, 2 * f)
    return pl.pallas_call(
        functools.partial(_ffn_rows_kernel, tf=tf, blocks_per_seq=blocks_per_seq),
        grid=(nblk,),
        in_specs=[row, const(g_pre), _layer_spec(wu_bf, layer), const(conv_w), const(conv_b),
                  _layer_spec(wd_bf, layer),
                  pl.BlockSpec((1, halo, 2 * f), lambda i: (i // blocks_per_seq, 0, 0)), const(g_post)],
        out_specs=[row, pl.BlockSpec((1, halo, 2 * f), lambda i: (i, 0, 0))],
        out_shape=[jax.ShapeDtypeStruct((m, d), F32), jax.ShapeDtypeStruct((nblk, halo, 2 * f), F32)],
        scratch_shapes=[pltpu.VMEM((tm, d), BF16), pltpu.VMEM((tm, f), BF16), pltpu.VMEM((halo, 2 * f), F32)],
        compiler_params=_cparams(("arbitrary",)),
        name="conv_ffn_rows",
    )(x, g_pre, wu_bf, conv_w, conv_b, wd_bf, hist, g_post)


def _conv_ffn(x, g_pre, wu_bf, conv_w, conv_b, wd_bf, layer, hist, g_post, *, tm, tf, shift, blocks_per_seq):
    m, d = x.shape
    assert shift % 8 == 0 and tm >= 2 * shift
    f = wd_bf.shape[1]
    nj = f // tf
    nblk = m // tm
    halo = 2 * shift
    up = lambda off: pl.BlockSpec((None, d, tf), lambda i, j: (layer, 0, off + j))
    cw = lambda off: pl.BlockSpec((CONV_WIDTH, tf), lambda i, j: (0, off + j))
    cb = lambda off: pl.BlockSpec((1, tf), lambda i, j: (0, off + j))
    hs = lambda off: pl.BlockSpec((1, halo, tf), lambda i, j: (i // blocks_per_seq, 0, off + j))
    vec = pl.BlockSpec((1, d), lambda i, j: (0, 0))
    row = pl.BlockSpec((tm, d), lambda i, j: (i, 0))
    tail = pl.BlockSpec((1, halo, tf), lambda i, j: (i, 0, j))
    return pl.pallas_call(
        functools.partial(_ffn_kernel, shift=shift, blocks_per_seq=blocks_per_seq),
        grid=(nblk, nj),
        in_specs=[row, vec, up(0), up(nj), cw(0), cw(nj), cb(0), cb(nj),
                  pl.BlockSpec((None, tf, d), lambda i, j: (layer, j, 0)), hs(0), hs(nj), vec],
        out_specs=[row, tail, tail],
        out_shape=[jax.ShapeDtypeStruct((m, d), F32),
                   jax.ShapeDtypeStruct((nblk, halo, f), F32),
                   jax.ShapeDtypeStruct((nblk, halo, f), F32)],
        scratch_shapes=[pltpu.VMEM((tm, d), BF16), pltpu.VMEM((tm, d), F32),
                        pltpu.VMEM((nj, halo, tf), F32), pltpu.VMEM((nj, halo, tf), F32)],
        compiler_params=_cparams(("arbitrary", "arbitrary")),
        name="conv_ffn",
    )(x, g_pre.reshape(1, d), wu_bf, wu_bf, conv_w, conv_w, conv_b.reshape(1, -1), conv_b.reshape(1, -1), wd_bf,
      hist, hist, g_post.reshape(1, d))


def _rope_tables(pos):
    inv = 1.0 / (ROPE_THETA ** (jnp.arange(0, HEAD_DIM, 2, dtype=F32) / HEAD_DIM))
    ang = pos.astype(F32)[:, None] * inv[None, :]
    ang = jnp.concatenate([ang, ang], axis=-1)
    cos, sin = jnp.cos(ang), jnp.sin(ang)
    first_half = jnp.arange(HEAD_DIM) < HEAD_DIM // 2
    sin_a = jnp.where(first_half[None], -sin, 0.0)
    sin_b = jnp.where(first_half[None], 0.0, sin)
    rep = LANES // HEAD_DIM
    return tuple(jnp.tile(t, (1, rep)) for t in (cos, sin_a, sin_b))


def _tile_for(n, target):
    t = min(n, target)
    assert n % t == 0, (n, t)
    return t


def kernel(x_prompt, x_sample, mem_prompt, state_ret, cache_diff_k, cache_diff_v, page_table, cache_mem_k, cache_mem_v, state_conv, g_mix_pre, g_mix_post, w_in, ret_norm_g, lam_q1, lam_k1, lam_q2, lam_k2, diff_norm_g, w_out, g_mem_pre, g_mem_post, g_mem_kv, w_mq, w_mk, w_mv, w_mo, g_ffn_pre, g_ffn_post, w_up, conv_w, conv_b, w_down):
    B, S, D = x_prompt.shape
    DB, DS, _ = x_sample.shape
    depth = w_in.shape[0]
    n_pool, page = cache_diff_k.shape[1], cache_diff_k.shape[2]
    past = page_table.shape[1] * page
    n_mem = mem_prompt.shape[1]
    two_f = w_up.shape[2]
    mem_hd = D // N_MEM_HEADS
    mp, ms = B * S, DB * DS

    tm_p = _tile_for(S, 512)
    tile_attn = _tile_for(S, 512)
    tf = 256
    assert (two_f // 2) % tf == 0
    assert mem_hd == 2 * LANES

    tabs_p = _rope_tables(jnp.arange(S, dtype=jnp.int32))
    tabs_s = _rope_tables(jnp.tile(past + jnp.arange(DS, dtype=jnp.int32), DB))
    no_tabs_mem = tuple(jnp.zeros((B * n_mem, LANES), F32) for _ in range(3))
    no_tabs_s = tuple(jnp.zeros((ms, LANES), F32) for _ in range(3))

    kc = cache_diff_k.reshape(depth, n_pool, page * N_DIFF_HEADS, DIFF_DV)
    vc = cache_diff_v.reshape(depth, n_pool, page * N_DIFF_HEADS, DIFF_DV)

    def mem_rows(c):
        c = c.reshape(depth, DB, n_mem, N_MEM_HEADS, 2, LANES).transpose(0, 1, 2, 4, 3, 5)
        return c.reshape(depth, DB, n_mem * 2 * N_MEM_HEADS, LANES)

    mkc = mem_rows(cache_mem_k)
    mvc = mem_rows(cache_mem_v)

    xp = x_prompt.reshape(mp, D)
    xs = x_sample.reshape(ms, D)
    mem2d = mem_prompt.reshape(B * n_mem, D)
    zero_hist = jnp.zeros((B, 2, two_f), F32)
    ret_pad = 8

    outs = {k: [] for k in ("p_ret", "p_dk", "p_dv", "p_mk", "p_mv", "p_conv", "s_ret", "s_dk", "s_dv", "s_conv")}
    rope_secs = (0, 1, 4, 5)
    w_in_bf = w_in.astype(BF16)
    w_out_bf = w_out.astype(BF16)
    w_mq_bf = w_mq.astype(BF16)
    w_mo_bf = w_mo.astype(BF16)
    w_mkv_bf = jnp.concatenate([w_mk, w_mv], axis=2).astype(BF16)
    w_up_bf = w_up.astype(BF16)
    w_down_bf = w_down.astype(BF16)
    for l in range(depth):
        lam_init = 0.8 - 0.6 * math.exp(-0.3 * l)
        lam_params = [t[l].reshape(1, HEAD_DIM) for t in (lam_q1, lam_k1, lam_q2, lam_k2)]

        z, dk, dv = _project(xp, g_mix_pre[l], w_in_bf, l, tabs_p, tm=tm_p, tn=SECTION, rope_sections=rope_secs,
                             k_section=1, n_main=5)
        ro, s_fin = _retention_prompt(z, ret_norm_g[l], B, S, rows_per_step=tm_p)
        do = _diff_prompt(z, dk, dv, lam_params, diff_norm_g[l], lam_init, B, S, tile=tile_attn, heads=4)
        xp = _merge(xp, z, ro, do, w_out_bf, l, g_mix_post[l], tm=tm_p)
        outs["p_ret"].append(s_fin)
        outs["p_dk"].append(dk.reshape(B, S, N_DIFF_HEADS, DIFF_DV))
        outs["p_dv"].append(dv.reshape(B, S, N_DIFF_HEADS, DIFF_DV))

        zs, dqs, dks, dvs = _project(xs, g_mix_pre[l], w_in_bf, l, tabs_s, tm=ms, tn=SECTION,
                                     rope_sections=rope_secs, k_section=1, n_main=4)

        def heads_first(t):
            t = t.reshape(DB, DS, N_RET_HEADS, HEAD_DIM).transpose(0, 2, 1, 3)
            return jnp.pad(t, ((0, 0), (0, 0), (0, ret_pad - DS), (0, 0)))

        ro_s, s_new = _retention_sample(heads_first(zs[:, :SECTION]), heads_first(zs[:, SECTION:2 * SECTION]),
                                        heads_first(zs[:, 2 * SECTION:3 * SECTION]), state_ret[l], ret_norm_g[l],
                                        DS, bb=8)
        ro_s = ro_s[:, :, :DS].transpose(0, 2, 1, 3).reshape(ms, SECTION)

        g_rows = jnp.tile(diff_norm_g[l], (DS, 1))
        token_head_rows = lambda t: t.reshape(DB, DS * N_DIFF_HEADS, DIFF_DV)
        do_s = _diff_sample(token_head_rows(dqs), token_head_rows(dks), token_head_rows(dvs), kc, vc, page_table, l,
                            lam_params, g_rows, lam_init)
        xs = _merge(xs, zs, ro_s, do_s.reshape(ms, N_DIFF_HEADS * DIFF_DV), w_out_bf, l, g_mix_post[l], tm=ms)
        outs["s_ret"].append(s_new)
        outs["s_dk"].append(dks.reshape(DB, DS, N_DIFF_HEADS, DIFF_DV))
        outs["s_dv"].append(dvs.reshape(DB, DS, N_DIFF_HEADS, DIFF_DV))

        mk, mv = _project(mem2d, g_mem_kv[l], w_mkv_bf, l, no_tabs_mem, tm=B * n_mem, tn=D, n_main=1)
        mk = mk.reshape(B, n_mem, D)
        mv = mv.reshape(B, n_mem, D)
        xp = _mem_prompt(xp, g_mem_pre[l], w_mq_bf, mk, mv, w_mo_bf, l, g_mem_post[l], S, tm=tm_p)
        qs, = _project(xs, g_mem_pre[l], w_mq_bf, l, no_tabs_s, tm=ms, tn=SECTION)
        q_rows = qs.reshape(DB, DS * N_MEM_HEADS, 2, LANES).transpose(0, 2, 1, 3)
        om = _mem_sample(q_rows.reshape(DB, 2 * DS * N_MEM_HEADS, LANES), mkc, mvc, l, bb=4)
        xs = _out_proj(xs, om.reshape(ms, D), w_mo_bf, l, g_mem_post[l], tm=ms)
        outs["p_mk"].append(mk.reshape(B, n_mem, N_MEM_HEADS, mem_hd))
        outs["p_mv"].append(mv.reshape(B, n_mem, N_MEM_HEADS, mem_hd))

        xp, tails = _conv_ffn_rows(xp, g_ffn_pre[l], w_up_bf, conv_w[l], conv_b[l], w_down_bf, l, zero_hist,
                                   g_ffn_post[l], tm=tm_p, tf=tf, blocks_per_seq=S // tm_p)
        outs["p_conv"].append(tails[jnp.arange(B) * (S // tm_p) + (S // tm_p - 1)])
        xs_t = xs.reshape(DB, DS, D).transpose(1, 0, 2).reshape(ms, D)
        hist_s = state_conv[l].transpose(1, 0, 2).reshape(1, 2 * DB, two_f)
        xs_t, tg, tv = _conv_ffn(xs_t, g_ffn_pre[l], w_up_bf, conv_w[l], conv_b[l], w_down_bf, l, hist_s,
                                 g_ffn_post[l], tm=ms, tf=tf, shift=DB, blocks_per_seq=1)
        xs = xs_t.reshape(DS, DB, D).transpose(1, 0, 2).reshape(ms, D)
        outs["s_conv"].append(jnp.concatenate([tg[0], tv[0]], axis=-1).reshape(2, DB, two_f).transpose(1, 0, 2))

    st = lambda k: jnp.stack(outs[k])
    return (xp.reshape(B, S, D), xs.reshape(DB, DS, D), st("p_ret"), st("p_dk"), st("p_dv"), st("p_mk"), st("p_mv"),
            st("p_conv"), st("s_ret"), st("s_dk"), st("s_dv"), st("s_conv"))
```

```python
import functools
import math

import jax
import jax.numpy as jnp
from jax import lax
from jax.experimental import pallas as pl
from jax.experimental.pallas import tpu as pltpu

F32 = jnp.float32
BF16 = jnp.bfloat16

HEAD_DIM = 64
N_RET_HEADS = 8
N_DIFF_HEADS = 4
DIFF_DV = 2 * HEAD_DIM
SECTION = 512
N_SECTIONS = 7
N_MEM_HEADS = 4
CONV_WIDTH = 3
ROPE_THETA = 10000.0
EPS = 1e-6
NEG_BIG = -1e30
LOG2_E = math.log2(math.e)
RET_CHUNK = 128

LANES = 128
BF16_SUBLANES = 16
ONES_ROWS = BF16_SUBLANES
V7X_VMEM_BYTES = 64 * 1024 * 1024
VMEM_LIMIT = (V7X_VMEM_BYTES * 3) // 4


def _cparams(semantics):
    return pltpu.CompilerParams(dimension_semantics=semantics, vmem_limit_bytes=VMEM_LIMIT)


def _layer_spec(w, layer):
    zeros = (0,) * (w.ndim - 1)
    return pl.BlockSpec((None,) + w.shape[1:], lambda *_: (layer,) + zeros, pipeline_mode=pl.Buffered(1))


def _rms(x, g):
    return x * lax.rsqrt(jnp.mean(x * x, axis=-1, keepdims=True) + EPS) * g


def _dot(a, b):
    return jnp.dot(a, b, preferred_element_type=F32)


def _dot_nt(a, b):
    return lax.dot_general(a, b, (((1,), (1,)), ((), ())), preferred_element_type=F32)


def _silu(x):
    return x * (1.0 / (1.0 + jnp.exp(-x)))


def _proj_kernel(x_ref, g_ref, w_ref, cos_ref, sa_ref, sb_ref, *out_and_scratch, rope_sections, k_section,
                 n_main, tn):
    outs, xn_ref = out_and_scratch[:-1], out_and_scratch[-1]
    xn_ref[...] = _rms(x_ref[...], g_ref[...]).astype(BF16)
    n_sections = w_ref.shape[1] // tn

    def matmul(sec):
        return _dot(xn_ref[...], w_ref[:, sec * tn:(sec + 1) * tn])

    def emit(sec, acc):
        o_ref, col0 = (outs[0], sec * tn) if sec < n_main else (outs[sec - n_main + 1], 0)
        rope = sec in rope_sections
        scale = HEAD_DIM ** -0.5 if sec == k_section else 1.0
        if rope:
            c, sa, sb = cos_ref[...], sa_ref[...], sb_ref[...]
        for t in range(tn // LANES):
            r = acc[:, t * LANES:(t + 1) * LANES]
            if rope:
                r = r * c + pltpu.roll(r, LANES - HEAD_DIM // 2, 1) * sa + pltpu.roll(r, HEAD_DIM // 2, 1) * sb
                r = r * scale if scale != 1.0 else r
            o_ref[:, col0 + t * LANES:col0 + (t + 1) * LANES] = r

    acc_next = matmul(0)
    for sec in range(n_sections):
        acc = acc_next
        if sec + 1 < n_sections:
            acc_next = matmul(sec + 1)
        emit(sec, acc)


def _project(x, g, w_bf, layer, tabs, *, tm, tn, rope_sections=(), k_section=-1, n_main=None):
    m, d = x.shape
    n = w_bf.shape[2]
    n_sec = n // tn
    n_main = n_sec if n_main is None else n_main
    cos, sa, sb = tabs
    tab_blocks = cos.shape[0] // tm
    tab_spec = pl.BlockSpec((tm, LANES), lambda i: (i % tab_blocks, 0))
    out_shape = [jax.ShapeDtypeStruct((m, n_main * tn), F32)]
    out_specs = [pl.BlockSpec((tm, n_main * tn), lambda i: (i, 0))]
    for _ in range(n_main, n_sec):
        out_shape.append(jax.ShapeDtypeStruct((m, tn), F32))
        out_specs.append(pl.BlockSpec((tm, tn), lambda i: (i, 0)))
    kern = functools.partial(_proj_kernel, rope_sections=tuple(rope_sections), k_section=k_section, n_main=n_main,
                             tn=tn)
    return pl.pallas_call(
        kern,
        grid=(m // tm,),
        in_specs=[pl.BlockSpec((tm, d), lambda i: (i, 0)),
                  pl.BlockSpec((1, d), lambda i: (0, 0)),
                  _layer_spec(w_bf, layer),
                  tab_spec, tab_spec, tab_spec],
        out_specs=out_specs,
        out_shape=out_shape,
        scratch_shapes=[pltpu.VMEM((tm, d), BF16)],
        compiler_params=_cparams(("parallel",)),
        name="project",
    )(x, g.reshape(1, d), w_bf, cos, sa, sb)


def _ret_prompt_kernel(q_ref, k_ref, v_ref, dec_ref, rw_ref, ww_ref, cd_ref, g_ref, o_ref, sfin_ref, s_ref, *,
                       n_chunks):
    r = pl.program_id(1)

    @pl.when(r == 0)
    def _():
        s_ref[...] = jnp.zeros_like(s_ref)

    lane = lax.broadcasted_iota(jnp.int32, (RET_CHUNK, LANES), 1)
    first = lane < HEAD_DIM
    srow = lax.broadcasted_iota(jnp.int32, (LANES, LANES), 0)
    scol = lax.broadcasted_iota(jnp.int32, (LANES, LANES), 1)
    same_head = (srow < HEAD_DIM) == (scol < HEAD_DIM)

    def half_mean(t):
        tot = jnp.sum(t, axis=-1, keepdims=True)
        lo = jnp.sum(jnp.where(first, t, 0.0), axis=-1, keepdims=True)
        return jnp.where(first, lo, tot - lo) * (1.0 / HEAD_DIM)

    def chunk(c, carry):
        row = pl.multiple_of(c * RET_CHUNK, RET_CHUNK)
        rows = pl.ds(row, RET_CHUNK)
        for p in range(N_RET_HEADS // 2):
            cols = slice(p * LANES, (p + 1) * LANES)
            q = q_ref[rows, cols]
            k = k_ref[rows, cols]
            vb = v_ref[rows, cols].astype(BF16)
            kb = k.astype(BF16)
            s = s_ref[p]
            sc_a = (_dot_nt(jnp.where(first, q, 0.0).astype(BF16), kb) * dec_ref[2 * p]).astype(BF16)
            sc_b = (_dot_nt(jnp.where(first, 0.0, q).astype(BF16), kb) * dec_ref[2 * p + 1]).astype(BF16)
            o = jnp.where(first, _dot(sc_a, vb), _dot(sc_b, vb)) + _dot(q.astype(BF16), s.astype(BF16)) * rw_ref[p]
            kw_t = (k * ww_ref[p]).T.astype(BF16)
            s_ref[p] = s * cd_ref[p] + jnp.where(same_head, _dot(kw_t, vb), 0.0)
            d = o - half_mean(o)
            o_ref[rows, cols] = d * lax.rsqrt(half_mean(d * d) + EPS) * g_ref[p]
        return carry

    lax.fori_loop(0, n_chunks, chunk, 0)

    @pl.when(r == pl.num_programs(1) - 1)
    def _():
        for h in range(N_RET_HEADS):
            lo = (h % 2) * HEAD_DIM
            sfin_ref[0, h] = s_ref[h // 2][lo:lo + HEAD_DIM, lo:lo + HEAD_DIM]


def _retention_tables(length):
    log_g = jnp.log1p(-jnp.exp2(-5.0 - jnp.arange(N_RET_HEADS, dtype=F32)))
    n = jnp.arange(length, dtype=F32)
    diff = n[:, None] - n[None, :]
    dec = jnp.where((diff >= 0)[None], jnp.exp(log_g[:, None, None] * jnp.maximum(diff, 0.0)[None]), 0.0)
    read_w = jnp.exp(log_g[:, None] * (n + 1.0)[None])
    write_w = jnp.exp(log_g[:, None] * (length - 1.0 - n)[None])
    chunk_decay = jnp.exp(log_g * length)
    return dec, read_w, write_w, chunk_decay


def _retention_prompt(z, ret_g, batch, seq, *, rows_per_step):
    m = z.shape[0]
    nr = seq // rows_per_step
    dec, read_w, write_w, chunk_decay = _retention_tables(RET_CHUNK)
    n_pairs = N_RET_HEADS // 2

    def per_lane(t):
        t = jnp.broadcast_to(t[:, :, None], (N_RET_HEADS, t.shape[1], HEAD_DIM))
        return t.reshape(n_pairs, 2, t.shape[1], HEAD_DIM).transpose(0, 2, 1, 3).reshape(n_pairs, t.shape[1], LANES)

    rw = per_lane(read_w)
    ww = per_lane(write_w)
    cd = jnp.repeat(per_lane(jnp.broadcast_to(chunk_decay[:, None], (N_RET_HEADS, HEAD_DIM))), 2, axis=1)
    full3 = lambda shape: pl.BlockSpec(shape, lambda b, r: (0, 0, 0))
    sec = lambda c: pl.BlockSpec((rows_per_step, SECTION), lambda b, r: (b * nr + r, c))
    return pl.pallas_call(
        functools.partial(_ret_prompt_kernel, n_chunks=rows_per_step // RET_CHUNK),
        grid=(batch, nr),
        in_specs=[sec(0), sec(1), sec(2),
                  full3((N_RET_HEADS, RET_CHUNK, RET_CHUNK)),
                  full3((n_pairs, RET_CHUNK, LANES)),
                  full3((n_pairs, RET_CHUNK, LANES)),
                  full3((n_pairs, LANES, LANES)),
                  full3((n_pairs, 1, LANES))],
        out_specs=[pl.BlockSpec((rows_per_step, SECTION), lambda b, r: (b * nr + r, 0)),
                   pl.BlockSpec((1, N_RET_HEADS, HEAD_DIM, HEAD_DIM), lambda b, r: (b, 0, 0, 0))],
        out_shape=[jax.ShapeDtypeStruct((m, SECTION), F32),
                   jax.ShapeDtypeStruct((batch, N_RET_HEADS, HEAD_DIM, HEAD_DIM), F32)],
        scratch_shapes=[pltpu.VMEM((n_pairs, LANES, LANES), F32)],
        compiler_params=_cparams(("parallel", "arbitrary")),
        name="retention_prompt",
    )(z, z, z, dec, rw, ww, cd, ret_g.reshape(n_pairs, 1, LANES))


def _ret_sample_kernel(q_ref, k_ref, v_ref, s_ref, dec_ref, rw_ref, ww_ref, cd_ref, g_ref, *rest):
    o_ref, snew_ref = rest[-2:]
    n = q_ref.shape[0] * q_ref.shape[1]
    length = q_ref.shape[2]
    q = q_ref[...].reshape(n, length, HEAD_DIM)
    k = k_ref[...].reshape(n, length, HEAD_DIM)
    vb = v_ref[...].reshape(n, length, HEAD_DIM).astype(BF16)
    s = s_ref[...].reshape(n, HEAD_DIM, HEAD_DIM)
    qb = q.astype(BF16)
    scores = jnp.einsum('nqd,nkd->nqk', qb, k.astype(BF16), preferred_element_type=F32) * dec_ref[...]
    o = jnp.einsum('nqk,nke->nqe', scores.astype(BF16), vb, preferred_element_type=F32)
    o = o + jnp.einsum('nqd,nde->nqe', qb, s.astype(BF16), preferred_element_type=F32) * rw_ref[...]
    kw = (k * ww_ref[...]).astype(BF16)
    s_new = s * cd_ref[...] + jnp.einsum('nld,nle->nde', kw, vb, preferred_element_type=F32)
    mu = jnp.mean(o, axis=-1, keepdims=True)
    var = jnp.mean(jnp.square(o - mu), axis=-1, keepdims=True)
    o_ref[...] = ((o - mu) * lax.rsqrt(var + EPS) * g_ref[...]).reshape(o_ref.shape)
    snew_ref[...] = s_new.reshape(snew_ref.shape)


def _retention_sample(q, k, v, states, layer, new_states, ret_g, n_tokens, *, bb):
    db, nh, length, _ = q.shape
    prior = () if new_states is None else (new_states,)
    dec, read_w, write_w, chunk_decay = _retention_tables(n_tokens)
    pad = length - n_tokens
    dec = jnp.pad(dec, ((0, 0), (0, pad), (0, pad)))
    read_w = jnp.pad(read_w, ((0, 0), (0, pad)))
    write_w = jnp.pad(write_w, ((0, 0), (0, pad)))
    n = bb * nh
    tile = lambda t: jnp.tile(t, (bb,) + (1,) * (t.ndim - 1))
    dec_t = tile(dec)
    rw_t = tile(jnp.broadcast_to(read_w[:, :, None], (nh, length, HEAD_DIM)))
    ww_t = tile(jnp.broadcast_to(write_w[:, :, None], (nh, length, HEAD_DIM)))
    cd_t = tile(jnp.broadcast_to(chunk_decay[:, None, None], (nh, HEAD_DIM, HEAD_DIM)))
    g_t = tile(jnp.broadcast_to(ret_g[:, None, :], (nh, length, HEAD_DIM)))
    blk = pl.BlockSpec((bb, nh, length, HEAD_DIM), lambda i: (i, 0, 0, 0))
    sblk = pl.BlockSpec((None, bb, nh, HEAD_DIM, HEAD_DIM), lambda i: (layer, i, 0, 0, 0))
    full = lambda a: pl.BlockSpec(a.shape, lambda i: (0,) * a.ndim)
    return pl.pallas_call(
        _ret_sample_kernel,
        grid=(db // bb,),
        in_specs=[blk, blk, blk, sblk, full(dec_t), full(rw_t), full(ww_t), full(cd_t), full(g_t)]
        + [pl.BlockSpec(memory_space=pl.ANY)] * len(prior),
        out_specs=[blk, sblk],
        out_shape=[jax.ShapeDtypeStruct(q.shape, F32), jax.ShapeDtypeStruct(states.shape, F32)],
        input_output_aliases={9: 1} if prior else {},
        compiler_params=_cparams(("parallel",)),
        name="retention_sample",
    )(q, k, v, states, dec_t, rw_t, ww_t, cd_t, g_t, *prior)


def _lambda(lq1_ref, lk1_ref, lq2_ref, lk2_ref, lam_init):
    e1 = jnp.exp(jnp.sum(lq1_ref[...] * lk1_ref[...], axis=-1, keepdims=True))
    e2 = jnp.exp(jnp.sum(lq2_ref[...] * lk2_ref[...], axis=-1, keepdims=True))
    return e1 - e2 + lam_init


def _diff_prompt_kernel(qi_ref, ki_ref, q_ref, k_ref, v_ref, lq1_ref, lk1_ref, lq2_ref, lk2_ref, g_ref, *rest,
                        lam_init, heads):
    o_ref, ko_ref, vo_ref, m_ref, acc_ref, qm_ref = rest[-6:]
    p = pl.program_id(2)
    qi = qi_ref[p]
    ki = ki_ref[p]
    tk = k_ref.shape[0]

    @pl.when(ki == 0)
    def _():
        m_ref[...] = jnp.full_like(m_ref, NEG_BIG)
        acc_ref[...] = jnp.zeros_like(acc_ref)
        lane = lax.broadcasted_iota(jnp.int32, (q_ref.shape[0], DIFF_DV), 1)
        for h in range(heads):
            q = q_ref[:, h * DIFF_DV:(h + 1) * DIFF_DV] * (HEAD_DIM ** -0.5 * LOG2_E)
            qm_ref[2 * h] = jnp.where(lane < HEAD_DIM, q, 0.0).astype(BF16)
            qm_ref[2 * h + 1] = jnp.where(lane < HEAD_DIM, 0.0, q).astype(BF16)

    def step(masked):
        ones = jnp.ones((ONES_ROWS, tk), BF16)
        kb = [None] * heads

        def scores(idx):
            h = idx // 2
            if kb[h] is None:
                kb[h] = k_ref[:, h * DIFF_DV:(h + 1) * DIFF_DV].astype(BF16)
            return _dot_nt(kb[h], qm_ref[idx])

        def accumulate(idx, s, v_ext):
            if masked:
                key = lax.broadcasted_iota(jnp.int32, s.shape, 0)
                qry = lax.broadcasted_iota(jnp.int32, s.shape, 1)
                s = jnp.where(key <= qry, s, NEG_BIG)
            m_old = m_ref[idx]
            m_new = jnp.maximum(m_old, jnp.max(s, axis=0, keepdims=True))
            alpha = jnp.exp2(m_old - m_new)
            e = jnp.exp2(s - m_new).astype(BF16)
            acc_ref[idx] = alpha * acc_ref[idx] + _dot(v_ext, e)
            m_ref[idx] = m_new

        n = 2 * heads
        s_next = scores(0)
        v_ext = None
        for idx in range(n):
            s_cur = s_next
            if idx + 1 < n:
                s_next = scores(idx + 1)
            if idx % 2 == 0:
                hcols = slice((idx // 2) * DIFF_DV, (idx // 2 + 1) * DIFF_DV)
                v_ext = jnp.concatenate([v_ref[:, hcols].T.astype(BF16), ones], axis=0)
            accumulate(idx, s_cur, v_ext)

    @pl.when(ki < qi)
    def _():
        step(False)

    @pl.when(ki == qi)
    def _():
        step(True)
        for h in range(heads):
            ko_ref[:, h, :] = k_ref[:, h * DIFF_DV:(h + 1) * DIFF_DV]
            vo_ref[:, h, :] = v_ref[:, h * DIFF_DV:(h + 1) * DIFF_DV]
        lam = _lambda(lq1_ref, lk1_ref, lq2_ref, lk2_ref, lam_init)
        for h in range(heads):
            a1 = acc_ref[2 * h]
            a2 = acc_ref[2 * h + 1]
            o_t = (a1[:DIFF_DV] / a1[DIFF_DV:DIFF_DV + 1] - lam * (a2[:DIFF_DV] / a2[DIFF_DV:DIFF_DV + 1]))
            o_ref[:, h * DIFF_DV:(h + 1) * DIFF_DV] = _rms(o_t.T, g_ref[h]) * (1.0 - lam_init)


def _diff_prompt(z, dk, dv, lam_params, diff_g, lam_init, batch, seq, layer, depth, stacks, *, tile, heads):
    m = z.shape[0]
    nt = seq // tile
    width = heads * DIFF_DV
    groups = N_DIFF_HEADS // heads
    assert groups == 1
    pairs = [(qi, ki) for qi in range(nt) for ki in range(qi + 1)]
    qi_tab = jnp.asarray([p[0] for p in pairs], jnp.int32)
    ki_tab = jnp.asarray([p[1] for p in pairs], jnp.int32)
    q_col0 = 4 * SECTION // width
    stacks = () if stacks is None else tuple(stacks)
    lam_spec = pl.BlockSpec((1, HEAD_DIM), lambda b, h, p, qi, ki: (0, 0))
    stack_spec = pl.BlockSpec((None, tile, N_DIFF_HEADS, DIFF_DV),
                              lambda b, h, p, qi, ki: (layer, b * nt + qi[p], 0, 0))
    grid_spec = pltpu.PrefetchScalarGridSpec(
        num_scalar_prefetch=2,
        grid=(batch, groups, len(pairs)),
        in_specs=[pl.BlockSpec((tile, width), lambda b, h, p, qi, ki: (b * nt + qi[p], q_col0 + h)),
                  pl.BlockSpec((tile, width), lambda b, h, p, qi, ki: (b * nt + ki[p], h)),
                  pl.BlockSpec((tile, width), lambda b, h, p, qi, ki: (b * nt + ki[p], h)),
                  lam_spec, lam_spec, lam_spec, lam_spec,
                  pl.BlockSpec((heads, 1, DIFF_DV), lambda b, h, p, qi, ki: (h, 0, 0))]
        + [pl.BlockSpec(memory_space=pl.ANY)] * len(stacks),
        out_specs=[pl.BlockSpec((tile, width), lambda b, h, p, qi, ki: (b * nt + qi[p], h)), stack_spec, stack_spec],
        scratch_shapes=[pltpu.VMEM((2 * heads, 1, tile), F32),
                        pltpu.VMEM((2 * heads, DIFF_DV + ONES_ROWS, tile), F32),
                        pltpu.VMEM((2 * heads, tile, DIFF_DV), BF16)],
    )
    stack_shape = jax.ShapeDtypeStruct((depth, m, N_DIFF_HEADS, DIFF_DV), F32)
    n_in = 2 + 3 + len(lam_params) + 1
    return pl.pallas_call(
        functools.partial(_diff_prompt_kernel, lam_init=lam_init, heads=heads),
        grid_spec=grid_spec,
        out_shape=[jax.ShapeDtypeStruct((m, N_DIFF_HEADS * DIFF_DV), F32), stack_shape, stack_shape],
        input_output_aliases={n_in + s: 1 + s for s in range(len(stacks))},
        compiler_params=_cparams(("parallel", "parallel", "arbitrary")),
        name="diff_attn_prompt",
    )(qi_tab, ki_tab, z, dk, dv, *lam_params, diff_g.reshape(N_DIFF_HEADS, 1, DIFF_DV), *stacks)


def _diff_sample_kernel(pt_ref, q_ref, kn_ref, vn_ref, *rest, n_pages, lam_init):
    k_pages = rest[:n_pages]
    v_pages = rest[n_pages:2 * n_pages]
    g_ref, lq1_ref, lk1_ref, lq2_ref, lk2_ref, o_ref, s_ref = rest[2 * n_pages:]
    half = q_ref.shape[1]
    n_rows = 2 * half
    page_rows = k_pages[0].shape[0]
    qf = q_ref[0] * (HEAD_DIM ** -0.5)
    lane = lax.broadcasted_iota(jnp.int32, qf.shape, 1)
    q = jnp.concatenate([jnp.where(lane < HEAD_DIM, qf, 0.0), jnp.where(lane < HEAD_DIM, 0.0, qf)],
                        axis=0).astype(BF16)

    row_h = lax.broadcasted_iota(jnp.int32, (n_rows, page_rows), 0) % N_DIFF_HEADS
    col_h = lax.broadcasted_iota(jnp.int32, (n_rows, page_rows), 1) % N_DIFF_HEADS
    same_head = row_h == col_h
    m = jnp.full((n_rows, 1), NEG_BIG, F32)
    for p in range(n_pages):
        s = jnp.where(same_head, _dot_nt(q, k_pages[p][...].astype(BF16)), NEG_BIG)
        s_ref[:, p * page_rows:(p + 1) * page_rows] = s
        m = jnp.maximum(m, jnp.max(s, axis=-1, keepdims=True))

    n_new = kn_ref.shape[1]
    r = lax.broadcasted_iota(jnp.int32, (n_rows, n_new), 0)
    c = lax.broadcasted_iota(jnp.int32, (n_rows, n_new), 1)
    ok = (r % N_DIFF_HEADS == c % N_DIFF_HEADS) & (c // N_DIFF_HEADS <= (r % half) // N_DIFF_HEADS)
    s_new = jnp.where(ok, _dot_nt(q, kn_ref[0].astype(BF16)), NEG_BIG)
    m = jnp.maximum(m, jnp.max(s_new, axis=-1, keepdims=True))

    e_new = jnp.exp(s_new - m)
    l = jnp.sum(e_new, axis=-1, keepdims=True)
    for p in range(n_pages):
        cols = slice(p * page_rows, (p + 1) * page_rows)
        e = jnp.exp(s_ref[:, cols] - m)
        s_ref[:, cols] = e
        l = l + jnp.sum(e, axis=-1, keepdims=True)

    lam = _lambda(lq1_ref, lk1_ref, lq2_ref, lk2_ref, lam_init)
    inv = 1.0 / l
    w1 = inv[:half]
    w2 = lam * inv[half:]
    a_new = (e_new[:half] * w1 - e_new[half:] * w2).astype(BF16)
    o = _dot(a_new, vn_ref[0].astype(BF16))
    for p in range(n_pages):
        cols = slice(p * page_rows, (p + 1) * page_rows)
        a = (s_ref[:half, cols] * w1 - s_ref[half:, cols] * w2).astype(BF16)
        o = o + _dot(a, v_pages[p][...].astype(BF16))
    o_ref[0] = _rms(o, g_ref[...]) * (1.0 - lam_init)


def _diff_sample(q_new, k_new, v_new, k_cache, v_cache, page_table, layer, lam_params, g_rows, lam_init):
    db = q_new.shape[0]
    n_rows = 2 * q_new.shape[1]
    n_pages = page_table.shape[1]
    page_rows = k_cache.shape[2]
    n_new = k_new.shape[1]

    def page_spec(p):
        return pl.BlockSpec((None, None, page_rows, LANES), lambda b, pt: (layer, pt[b * n_pages + p], 0, 0))

    small = lambda rows: pl.BlockSpec((1, rows, LANES), lambda b, pt: (b, 0, 0))
    lam_spec = pl.BlockSpec((1, HEAD_DIM), lambda b, pt: (0, 0))
    grid_spec = pltpu.PrefetchScalarGridSpec(
        num_scalar_prefetch=1,
        grid=(db,),
        in_specs=[small(n_new), small(n_new), small(n_new)]
        + [page_spec(p) for p in range(n_pages)] * 2
        + [pl.BlockSpec((n_rows // 2, LANES), lambda b, pt: (0, 0)), lam_spec, lam_spec, lam_spec, lam_spec],
        out_specs=small(n_rows // 2),
        scratch_shapes=[pltpu.VMEM((n_rows, n_pages * page_rows), F32)],
    )
    return pl.pallas_call(
        functools.partial(_diff_sample_kernel, n_pages=n_pages, lam_init=lam_init),
        grid_spec=grid_spec,
        out_shape=jax.ShapeDtypeStruct((db, n_rows // 2, LANES), F32),
        compiler_params=_cparams(("parallel",)),
        name="diff_attn_sample",
    )(page_table.reshape(-1), q_new, k_new, v_new, *([k_cache] * n_pages), *([v_cache] * n_pages), g_rows,
      *lam_params)


def _merge_kernel(x_ref, rg_ref, ro_ref, do_ref, w_ref, g_ref, o_ref):
    a = jnp.concatenate([_silu(rg_ref[...]) * ro_ref[...], do_ref[...]], axis=-1).astype(BF16)
    o_ref[...] = x_ref[...] + _rms(_dot(a, w_ref[...]), g_ref[...])


def _merge(x, z, ro, do, w_bf, layer, g, *, tm):
    m, d = x.shape
    rg_col = 3
    row = lambda w: pl.BlockSpec((tm, w), lambda i: (i, 0))
    return pl.pallas_call(
        _merge_kernel,
        grid=(m // tm,),
        in_specs=[row(d), pl.BlockSpec((tm, SECTION), lambda i: (i, rg_col)), row(SECTION), row(SECTION),
                  _layer_spec(w_bf, layer), pl.BlockSpec((1, d), lambda i: (0, 0))],
        out_specs=row(d),
        out_shape=jax.ShapeDtypeStruct((m, d), F32),
        compiler_params=_cparams(("parallel",)),
        name="mixer_merge",
    )(x, z, ro, do, w_bf, g.reshape(1, d))


def _mem_prompt_kernel(x_ref, gpre_ref, wq_ref, mk_ref, mv_ref, wo_ref, gpost_ref, o_ref):
    x = x_ref[...]
    q = _dot(_rms(x, gpre_ref[...]).astype(BF16), wq_ref[...])
    hd = q.shape[1] // N_MEM_HEADS
    outs = []
    for h in range(N_MEM_HEADS):
        cols = slice(h * hd, (h + 1) * hd)
        s = _dot_nt(q[:, cols].astype(BF16), mk_ref[0, :, cols].astype(BF16)) * (hd ** -0.5)
        e = jnp.exp(s - jnp.max(s, axis=-1, keepdims=True))
        p = e / jnp.sum(e, axis=-1, keepdims=True)
        outs.append(_dot(p.astype(BF16), mv_ref[0, :, cols].astype(BF16)))
    o = jnp.concatenate(outs, axis=-1).astype(BF16)
    o_ref[...] = x + _rms(_dot(o, wo_ref[...]), gpost_ref[...])


def _mem_prompt(x, g_pre, wq_bf, mk, mv, wo_bf, layer, g_post, seq, *, tm):
    m, d = x.shape
    n_mem = mk.shape[1]
    blocks_per_seq = seq // tm
    vec = pl.BlockSpec((1, d), lambda i: (0, 0))
    mem = pl.BlockSpec((1, n_mem, d), lambda i: (i // blocks_per_seq, 0, 0))
    row = pl.BlockSpec((tm, d), lambda i: (i, 0))
    return pl.pallas_call(
        _mem_prompt_kernel,
        grid=(m // tm,),
        in_specs=[row, vec, _layer_spec(wq_bf, layer), mem, mem, _layer_spec(wo_bf, layer), vec],
        out_specs=row,
        out_shape=jax.ShapeDtypeStruct((m, d), F32),
        compiler_params=_cparams(("parallel",)),
        name="mem_attn_prompt",
    )(x, g_pre.reshape(1, d), wq_bf, mk, mv, wo_bf, g_post.reshape(1, d))


def _mem_sample_kernel(q_ref, k_ref, v_ref, o_ref):
    bb, n_rows, _ = q_ref.shape
    half = n_rows // 2
    key_rows = k_ref.shape[1]
    group = 2 * N_MEM_HEADS
    row = lax.broadcasted_iota(jnp.int32, (half, key_rows), 0)
    col = lax.broadcasted_iota(jnp.int32, (half, key_rows), 1)
    own = (col % group) == (row % N_MEM_HEADS)
    scale = (2 * LANES) ** -0.5
    for b in range(bb):
        sp = _dot_nt(q_ref[b].astype(BF16), k_ref[b].astype(BF16))
        s = (sp[:half] + pltpu.roll(sp[half:], key_rows - N_MEM_HEADS, 1)) * scale
        s = jnp.where(own, s, NEG_BIG)
        e = jnp.exp(s - jnp.max(s, axis=-1, keepdims=True))
        p = e / jnp.sum(e, axis=-1, keepdims=True)
        p2 = jnp.concatenate([p, pltpu.roll(p, N_MEM_HEADS, 1)], axis=0).astype(BF16)
        o = _dot(p2, v_ref[b].astype(BF16))
        o_ref[b] = jnp.concatenate([o[:half], o[half:]], axis=-1)


def _mem_sample(q_rows, k_cache, v_cache, layer, *, bb):
    db, n_rows, _ = q_rows.shape
    key_rows = k_cache.shape[2]
    qspec = pl.BlockSpec((bb, n_rows, LANES), lambda i: (i, 0, 0))
    cspec = pl.BlockSpec((None, bb, key_rows, LANES), lambda i: (layer, i, 0, 0))
    return pl.pallas_call(
        _mem_sample_kernel,
        grid=(db // bb,),
        in_specs=[qspec, cspec, cspec],
        out_specs=pl.BlockSpec((bb, n_rows // 2, 2 * LANES), lambda i: (i, 0, 0)),
        out_shape=jax.ShapeDtypeStruct((db, n_rows // 2, 2 * LANES), F32),
        compiler_params=_cparams(("parallel",)),
        name="mem_attn_sample",
    )(q_rows, k_cache, v_cache)


def _out_proj_kernel(x_ref, a_ref, w_ref, g_ref, o_ref):
    o_ref[...] = x_ref[...] + _rms(_dot(a_ref[...].astype(BF16), w_ref[...]), g_ref[...])


def _out_proj(x, a, w_bf, layer, g, *, tm):
    m, d = x.shape
    row = pl.BlockSpec((tm, d), lambda i: (i, 0))
    return pl.pallas_call(
        _out_proj_kernel,
        grid=(m // tm,),
        in_specs=[row, pl.BlockSpec((tm, a.shape[1]), lambda i: (i, 0)), _layer_spec(w_bf, layer),
                  pl.BlockSpec((1, d), lambda i: (0, 0))],
        out_specs=row,
        out_shape=jax.ShapeDtypeStruct((m, d), F32),
        compiler_params=_cparams(("parallel",)),
        name="out_proj",
    )(x, a, w_bf, g.reshape(1, d))


def _ffn_kernel(x_ref, gpre_ref, wug_ref, wuv_ref, cwg_ref, cwv_ref, cbg_ref, cbv_ref, wd_ref, hg_ref, hv_ref,
                gpost_ref, o_ref, tg_ref, tv_ref, xn_ref, acc_ref, halo_g_ref, halo_v_ref, *, shift,
                blocks_per_seq):
    i = pl.program_id(0)
    j = pl.program_id(1)
    tm = x_ref.shape[0]

    @pl.when(j == 0)
    def _():
        xn_ref[...] = _rms(x_ref[...], gpre_ref[...]).astype(BF16)
        acc_ref[...] = jnp.zeros_like(acc_ref)

    seq_start = (i % blocks_per_seq) == 0

    def conv(w_ref, cw_ref, cb_ref, hist_ref, halo_ref, tail_ref):
        u = _dot(xn_ref[...], w_ref[...])
        prev = jnp.where(seq_start, hist_ref[0], halo_ref[j])
        ue = jnp.concatenate([prev, u], axis=0)
        u2 = ue[:tm]
        u1 = ue[shift:shift + tm]
        tail = u[tm - 2 * shift:]
        halo_ref[j] = tail
        tail_ref[0] = tail
        cw = cw_ref[...]
        return cb_ref[...] + cw[0:1] * u2 + cw[1:2] * u1 + cw[2:3] * u

    gate = conv(wug_ref, cwg_ref, cbg_ref, hg_ref, halo_g_ref, tg_ref)
    val = conv(wuv_ref, cwv_ref, cbv_ref, hv_ref, halo_v_ref, tv_ref)
    acc_ref[...] += _dot((_silu(gate) * val).astype(BF16), wd_ref[...])

    @pl.when(j == pl.num_programs(1) - 1)
    def _():
        o_ref[...] = x_ref[...] + _rms(acc_ref[...], gpost_ref[...])


def _ffn_rows_kernel(x_ref, gpre_ref, wu_ref, cw_ref, cb_ref, wd_ref, hist_ref, gpost_ref, o_ref, tail_ref,
                     xn_ref, h_ref, halo_ref, *, tf, blocks_per_seq):
    i = pl.program_id(0)
    tm = x_ref.shape[0]
    f = wd_ref.shape[0]
    n_tiles = f // tf
    xn_ref[...] = _rms(x_ref[...], gpre_ref[...]).astype(BF16)
    seq_start = (i % blocks_per_seq) == 0
    rows = lax.broadcasted_iota(jnp.int32, (tm, tf), 0)

    def up(col):
        return _dot(xn_ref[...], wu_ref[:, col:col + tf])

    def conv(u, col):
        cols = slice(col, col + tf)
        prev = jnp.where(seq_start, hist_ref[0, :, cols], halo_ref[:, cols])
        u1 = jnp.where(rows == 0, prev[1:2], pltpu.roll(u, 1, 0))
        u2 = jnp.where(rows == 0, prev[0:1], jnp.where(rows == 1, prev[1:2], pltpu.roll(u, 2, 0)))
        tail = u[tm - 2:]
        halo_ref[:, cols] = tail
        tail_ref[0, :, cols] = tail
        cw = cw_ref[:, cols]
        return cb_ref[:, cols] + cw[0:1] * u2 + cw[1:2] * u1 + cw[2:3] * u

    ug, uv = up(0), up(f)
    for t in range(n_tiles):
        cg, cv = ug, uv
        if t + 1 < n_tiles:
            ug, uv = up((t + 1) * tf), up(f + (t + 1) * tf)
        gate = conv(cg, t * tf)
        val = conv(cv, f + t * tf)
        h_ref[:, t * tf:(t + 1) * tf] = (_silu(gate) * val).astype(BF16)
    o_ref[...] = x_ref[...] + _rms(_dot(h_ref[...], wd_ref[...]), gpost_ref[...])


def _conv_ffn_rows(x, g_pre, wu_bf, conv_w, conv_b, wd_bf, layer, hist, g_post, *, tm, tf, blocks_per_seq):
    m, d = x.shape
    f = wd_bf.shape[1]
    nblk = m // tm
    halo = CONV_WIDTH - 1
    const = lambda a: pl.BlockSpec(a.shape, lambda i: (0,) * a.ndim, pipeline_mode=pl.Buffered(1))
    row = pl.BlockSpec((tm, d), lambda i: (i, 0))
    g_pre, g_post, conv_b = g_pre.reshape(1, d), g_post.reshape(1, d), conv_b.reshape(1, 2 * f)
    return pl.pallas_call(
        functools.partial(_ffn_rows_kernel, tf=tf, blocks_per_seq=blocks_per_seq),
        grid=(nblk,),
        in_specs=[row, const(g_pre), _layer_spec(wu_bf, layer), const(conv_w), const(conv_b),
                  _layer_spec(wd_bf, layer),
                  pl.BlockSpec((1, halo, 2 * f), lambda i: (i // blocks_per_seq, 0, 0)), const(g_post)],
        out_specs=[row, pl.BlockSpec((1, halo, 2 * f), lambda i: (i, 0, 0))],
        out_shape=[jax.ShapeDtypeStruct((m, d), F32), jax.ShapeDtypeStruct((nblk, halo, 2 * f), F32)],
        scratch_shapes=[pltpu.VMEM((tm, d), BF16), pltpu.VMEM((tm, f), BF16), pltpu.VMEM((halo, 2 * f), F32)],
        compiler_params=_cparams(("arbitrary",)),
        name="conv_ffn_rows",
    )(x, g_pre, wu_bf, conv_w, conv_b, wd_bf, hist, g_post)


def _conv_ffn(x, g_pre, wu_bf, conv_w, conv_b, wd_bf, layer, hist, g_post, *, tm, tf, shift, blocks_per_seq):
    m, d = x.shape
    assert shift % 8 == 0 and tm >= 2 * shift
    f = wd_bf.shape[1]
    nj = f // tf
    nblk = m // tm
    halo = 2 * shift
    up = lambda off: pl.BlockSpec((None, d, tf), lambda i, j: (layer, 0, off + j))
    cw = lambda off: pl.BlockSpec((CONV_WIDTH, tf), lambda i, j: (0, off + j))
    cb = lambda off: pl.BlockSpec((1, tf), lambda i, j: (0, off + j))
    hs = lambda off: pl.BlockSpec((1, halo, tf), lambda i, j: (i // blocks_per_seq, 0, off + j))
    vec = pl.BlockSpec((1, d), lambda i, j: (0, 0))
    row = pl.BlockSpec((tm, d), lambda i, j: (i, 0))
    tail = pl.BlockSpec((1, halo, tf), lambda i, j: (i, 0, j))
    return pl.pallas_call(
        functools.partial(_ffn_kernel, shift=shift, blocks_per_seq=blocks_per_seq),
        grid=(nblk, nj),
        in_specs=[row, vec, up(0), up(nj), cw(0), cw(nj), cb(0), cb(nj),
                  pl.BlockSpec((None, tf, d), lambda i, j: (layer, j, 0)), hs(0), hs(nj), vec],
        out_specs=[row, tail, tail],
        out_shape=[jax.ShapeDtypeStruct((m, d), F32),
                   jax.ShapeDtypeStruct((nblk, halo, f), F32),
                   jax.ShapeDtypeStruct((nblk, halo, f), F32)],
        scratch_shapes=[pltpu.VMEM((tm, d), BF16), pltpu.VMEM((tm, d), F32),
                        pltpu.VMEM((nj, halo, tf), F32), pltpu.VMEM((nj, halo, tf), F32)],
        compiler_params=_cparams(("arbitrary", "arbitrary")),
        name="conv_ffn",
    )(x, g_pre.reshape(1, d), wu_bf, wu_bf, conv_w, conv_w, conv_b.reshape(1, -1), conv_b.reshape(1, -1), wd_bf,
      hist, hist, g_post.reshape(1, d))


def _rope_tables(pos):
    inv = 1.0 / (ROPE_THETA ** (jnp.arange(0, HEAD_DIM, 2, dtype=F32) / HEAD_DIM))
    ang = pos.astype(F32)[:, None] * inv[None, :]
    ang = jnp.concatenate([ang, ang], axis=-1)
    cos, sin = jnp.cos(ang), jnp.sin(ang)
    first_half = jnp.arange(HEAD_DIM) < HEAD_DIM // 2
    sin_a = jnp.where(first_half[None], -sin, 0.0)
    sin_b = jnp.where(first_half[None], 0.0, sin)
    rep = LANES // HEAD_DIM
    return tuple(jnp.tile(t, (1, rep)) for t in (cos, sin_a, sin_b))


def _tile_for(n, target):
    t = min(n, target)
    assert n % t == 0, (n, t)
    return t


def kernel(x_prompt, x_sample, mem_prompt, state_ret, cache_diff_k, cache_diff_v, page_table, cache_mem_k, cache_mem_v, state_conv, g_mix_pre, g_mix_post, w_in, ret_norm_g, lam_q1, lam_k1, lam_q2, lam_k2, diff_norm_g, w_out, g_mem_pre, g_mem_post, g_mem_kv, w_mq, w_mk, w_mv, w_mo, g_ffn_pre, g_ffn_post, w_up, conv_w, conv_b, w_down):
    B, S, D = x_prompt.shape
    DB, DS, _ = x_sample.shape
    depth = w_in.shape[0]
    n_pool, page = cache_diff_k.shape[1], cache_diff_k.shape[2]
    past = page_table.shape[1] * page
    n_mem = mem_prompt.shape[1]
    two_f = w_up.shape[2]
    mem_hd = D // N_MEM_HEADS
    mp, ms = B * S, DB * DS

    tm_p = _tile_for(S, 512)
    tile_attn = _tile_for(S, 512)
    tf = 256
    assert (two_f // 2) % tf == 0
    assert mem_hd == 2 * LANES

    tabs_p = _rope_tables(jnp.arange(S, dtype=jnp.int32))
    tabs_s = _rope_tables(jnp.tile(past + jnp.arange(DS, dtype=jnp.int32), DB))
    no_tabs_mem = tuple(jnp.zeros((B * n_mem, LANES), F32) for _ in range(3))
    no_tabs_s = tuple(jnp.zeros((ms, LANES), F32) for _ in range(3))

    kc = cache_diff_k.reshape(depth, n_pool, page * N_DIFF_HEADS, DIFF_DV)
    vc = cache_diff_v.reshape(depth, n_pool, page * N_DIFF_HEADS, DIFF_DV)

    def mem_rows(c):
        c = c.reshape(depth, DB, n_mem, N_MEM_HEADS, 2, LANES).transpose(0, 1, 2, 4, 3, 5)
        return c.reshape(depth, DB, n_mem * 2 * N_MEM_HEADS, LANES)

    mkc = mem_rows(cache_mem_k)
    mvc = mem_rows(cache_mem_v)

    xp = x_prompt.reshape(mp, D)
    xs = x_sample.reshape(ms, D)
    mem2d = mem_prompt.reshape(B * n_mem, D)
    zero_hist = jnp.zeros((B, 2, two_f), F32)
    ret_pad = 8

    outs = {k: [] for k in ("p_ret", "p_mk", "p_mv", "p_conv", "s_dk", "s_dv", "s_conv")}
    kv_stacks = None
    s_ret = None
    rope_secs = (0, 1, 4, 5)
    w_in_bf = w_in.astype(BF16)
    w_out_bf = w_out.astype(BF16)
    w_mq_bf = w_mq.astype(BF16)
    w_mo_bf = w_mo.astype(BF16)
    w_mkv_bf = jnp.concatenate([w_mk, w_mv], axis=2).astype(BF16)
    w_up_bf = w_up.astype(BF16)
    w_down_bf = w_down.astype(BF16)
    for l in range(depth):
        lam_init = 0.8 - 0.6 * math.exp(-0.3 * l)
        lam_params = [t[l].reshape(1, HEAD_DIM) for t in (lam_q1, lam_k1, lam_q2, lam_k2)]

        z, dk, dv = _project(xp, g_mix_pre[l], w_in_bf, l, tabs_p, tm=tm_p, tn=SECTION, rope_sections=rope_secs,
                             k_section=1, n_main=5)
        ro, s_fin = _retention_prompt(z, ret_norm_g[l], B, S, rows_per_step=tm_p)
        do, *kv_stacks = _diff_prompt(z, dk, dv, lam_params, diff_norm_g[l], lam_init, B, S, l, depth, kv_stacks,
                                      tile=tile_attn, heads=N_DIFF_HEADS)
        xp = _merge(xp, z, ro, do, w_out_bf, l, g_mix_post[l], tm=tm_p)
        outs["p_ret"].append(s_fin)

        zs, dqs, dks, dvs = _project(xs, g_mix_pre[l], w_in_bf, l, tabs_s, tm=ms, tn=SECTION,
                                     rope_sections=rope_secs, k_section=1, n_main=4)

        def heads_first(t):
            t = t.reshape(DB, DS, N_RET_HEADS, HEAD_DIM).transpose(0, 2, 1, 3)
            return jnp.pad(t, ((0, 0), (0, 0), (0, ret_pad - DS), (0, 0)))

        ro_s, s_ret = _retention_sample(heads_first(zs[:, :SECTION]), heads_first(zs[:, SECTION:2 * SECTION]),
                                        heads_first(zs[:, 2 * SECTION:3 * SECTION]), state_ret, l, s_ret, ret_norm_g[l],
                                        DS, bb=8)
        ro_s = ro_s[:, :, :DS].transpose(0, 2, 1, 3).reshape(ms, SECTION)

        g_rows = jnp.tile(diff_norm_g[l], (DS, 1))
        token_head_rows = lambda t: t.reshape(DB, DS * N_DIFF_HEADS, DIFF_DV)
        do_s = _diff_sample(token_head_rows(dqs), token_head_rows(dks), token_head_rows(dvs), kc, vc, page_table, l,
                            lam_params, g_rows, lam_init)
        xs = _merge(xs, zs, ro_s, do_s.reshape(ms, N_DIFF_HEADS * DIFF_DV), w_out_bf, l, g_mix_post[l], tm=ms)
        outs["s_dk"].append(dks.reshape(DB, DS, N_DIFF_HEADS, DIFF_DV))
        outs["s_dv"].append(dvs.reshape(DB, DS, N_DIFF_HEADS, DIFF_DV))

        mk, mv = _project(mem2d, g_mem_kv[l], w_mkv_bf, l, no_tabs_mem, tm=B * n_mem, tn=D, n_main=1)
        mk = mk.reshape(B, n_mem, D)
        mv = mv.reshape(B, n_mem, D)
        xp = _mem_prompt(xp, g_mem_pre[l], w_mq_bf, mk, mv, w_mo_bf, l, g_mem_post[l], S, tm=tm_p)
        qs, = _project(xs, g_mem_pre[l], w_mq_bf, l, no_tabs_s, tm=ms, tn=SECTION)
        q_rows = qs.reshape(DB, DS * N_MEM_HEADS, 2, LANES).transpose(0, 2, 1, 3)
        om = _mem_sample(q_rows.reshape(DB, 2 * DS * N_MEM_HEADS, LANES), mkc, mvc, l, bb=4)
        xs = _out_proj(xs, om.reshape(ms, D), w_mo_bf, l, g_mem_post[l], tm=ms)
        outs["p_mk"].append(mk.reshape(B, n_mem, N_MEM_HEADS, mem_hd))
        outs["p_mv"].append(mv.reshape(B, n_mem, N_MEM_HEADS, mem_hd))

        xp, tails = _conv_ffn_rows(xp, g_ffn_pre[l], w_up_bf, conv_w[l], conv_b[l], w_down_bf, l, zero_hist,
                                   g_ffn_post[l], tm=tm_p, tf=tf, blocks_per_seq=S // tm_p)
        outs["p_conv"].append(tails[jnp.arange(B) * (S // tm_p) + (S // tm_p - 1)])
        xs_t = xs.reshape(DB, DS, D).transpose(1, 0, 2).reshape(ms, D)
        hist_s = state_conv[l].transpose(1, 0, 2).reshape(1, 2 * DB, two_f)
        xs_t, tg, tv = _conv_ffn(xs_t, g_ffn_pre[l], w_up_bf, conv_w[l], conv_b[l], w_down_bf, l, hist_s,
                                 g_ffn_post[l], tm=ms, tf=tf, shift=DB, blocks_per_seq=1)
        xs = xs_t.reshape(DS, DB, D).transpose(1, 0, 2).reshape(ms, D)
        outs["s_conv"].append(jnp.concatenate([tg[0], tv[0]], axis=-1).reshape(2, DB, two_f).transpose(1, 0, 2))

    st = lambda k: jnp.stack(outs[k])
    p_dk, p_dv = (t.reshape(depth, B, S, N_DIFF_HEADS, DIFF_DV) for t in kv_stacks)
    return (xp.reshape(B, S, D), xs.reshape(DB, DS, D), st("p_ret"), p_dk, p_dv, st("p_mk"), st("p_mv"),
            st("p_conv"), s_ret, st("s_dk"), st("s_dv"), st("s_conv"))
```

```python
import functools
import math

import jax
import jax.numpy as jnp
from jax import lax
from jax.experimental import pallas as pl
from jax.experimental.pallas import tpu as pltpu

F32 = jnp.float32
BF16 = jnp.bfloat16

HEAD_DIM = 64
N_RET_HEADS = 8
N_DIFF_HEADS = 4
DIFF_DV = 2 * HEAD_DIM
SECTION = 512
N_SECTIONS = 7
N_MEM_HEADS = 4
CONV_WIDTH = 3
ROPE_THETA = 10000.0
EPS = 1e-6
NEG_BIG = -1e30
LOG2_E = math.log2(math.e)
RET_CHUNK = 128

LANES = 128
BF16_SUBLANES = 16
ONES_ROWS = BF16_SUBLANES
V7X_VMEM_BYTES = 64 * 1024 * 1024
VMEM_LIMIT = (V7X_VMEM_BYTES * 3) // 4


def _cparams(semantics):
    return pltpu.CompilerParams(dimension_semantics=semantics, vmem_limit_bytes=VMEM_LIMIT)


def _layer_spec(w, layer):
    zeros = (0,) * (w.ndim - 1)
    return pl.BlockSpec((None,) + w.shape[1:], lambda *_: (layer,) + zeros, pipeline_mode=pl.Buffered(1))


def _rms(x, g):
    return x * lax.rsqrt(jnp.mean(x * x, axis=-1, keepdims=True) + EPS) * g


def _dot(a, b):
    return jnp.dot(a, b, preferred_element_type=F32)


def _dot_nt(a, b):
    return lax.dot_general(a, b, (((1,), (1,)), ((), ())), preferred_element_type=F32)


def _silu(x):
    return x * (1.0 / (1.0 + jnp.exp(-x)))


def _proj_kernel(x_ref, g_ref, w_ref, cos_ref, sa_ref, sb_ref, *out_and_scratch, rope_sections, k_section,
                 n_main, tn):
    outs, xn_ref = out_and_scratch[:-1], out_and_scratch[-1]
    xn_ref[...] = _rms(x_ref[...], g_ref[...]).astype(BF16)
    n_sections = w_ref.shape[1] // tn

    def matmul(sec):
        return _dot(xn_ref[...], w_ref[:, sec * tn:(sec + 1) * tn])

    def emit(sec, acc):
        o_ref, col0 = (outs[0], sec * tn) if sec < n_main else (outs[sec - n_main + 1], 0)
        rope = sec in rope_sections
        scale = HEAD_DIM ** -0.5 if sec == k_section else 1.0
        if rope:
            c, sa, sb = cos_ref[...], sa_ref[...], sb_ref[...]
        for t in range(tn // LANES):
            r = acc[:, t * LANES:(t + 1) * LANES]
            if rope:
                r = r * c + pltpu.roll(r, LANES - HEAD_DIM // 2, 1) * sa + pltpu.roll(r, HEAD_DIM // 2, 1) * sb
                r = r * scale if scale != 1.0 else r
            o_ref[:, col0 + t * LANES:col0 + (t + 1) * LANES] = r

    acc_next = matmul(0)
    for sec in range(n_sections):
        acc = acc_next
        if sec + 1 < n_sections:
            acc_next = matmul(sec + 1)
        emit(sec, acc)


def _project(x, g, w_bf, layer, tabs, *, tm, tn, rope_sections=(), k_section=-1, n_main=None):
    m, d = x.shape
    n = w_bf.shape[2]
    n_sec = n // tn
    n_main = n_sec if n_main is None else n_main
    cos, sa, sb = tabs
    tab_blocks = cos.shape[0] // tm
    tab_spec = pl.BlockSpec((tm, LANES), lambda i: (i % tab_blocks, 0))
    out_shape = [jax.ShapeDtypeStruct((m, n_main * tn), F32)]
    out_specs = [pl.BlockSpec((tm, n_main * tn), lambda i: (i, 0))]
    for _ in range(n_main, n_sec):
        out_shape.append(jax.ShapeDtypeStruct((m, tn), F32))
        out_specs.append(pl.BlockSpec((tm, tn), lambda i: (i, 0)))
    kern = functools.partial(_proj_kernel, rope_sections=tuple(rope_sections), k_section=k_section, n_main=n_main,
                             tn=tn)
    return pl.pallas_call(
        kern,
        grid=(m // tm,),
        in_specs=[pl.BlockSpec((tm, d), lambda i: (i, 0)),
                  pl.BlockSpec((1, d), lambda i: (0, 0)),
                  _layer_spec(w_bf, layer),
                  tab_spec, tab_spec, tab_spec],
        out_specs=out_specs,
        out_shape=out_shape,
        scratch_shapes=[pltpu.VMEM((tm, d), BF16)],
        compiler_params=_cparams(("parallel",)),
        name="project",
    )(x, g.reshape(1, d), w_bf, cos, sa, sb)


def _ret_prompt_kernel(q_ref, k_ref, v_ref, dec_ref, rw_ref, ww_ref, cd_ref, g_ref, o_ref, sfin_ref, s_ref, *,
                       n_chunks):
    r = pl.program_id(1)

    @pl.when(r == 0)
    def _():
        s_ref[...] = jnp.zeros_like(s_ref)

    lane = lax.broadcasted_iota(jnp.int32, (RET_CHUNK, LANES), 1)
    first = lane < HEAD_DIM
    srow = lax.broadcasted_iota(jnp.int32, (LANES, LANES), 0)
    scol = lax.broadcasted_iota(jnp.int32, (LANES, LANES), 1)
    same_head = (srow < HEAD_DIM) == (scol < HEAD_DIM)

    def half_mean(t):
        tot = jnp.sum(t, axis=-1, keepdims=True)
        lo = jnp.sum(jnp.where(first, t, 0.0), axis=-1, keepdims=True)
        return jnp.where(first, lo, tot - lo) * (1.0 / HEAD_DIM)

    def chunk(c, carry):
        row = pl.multiple_of(c * RET_CHUNK, RET_CHUNK)
        rows = pl.ds(row, RET_CHUNK)
        for p in range(N_RET_HEADS // 2):
            cols = slice(p * LANES, (p + 1) * LANES)
            q = q_ref[rows, cols]
            k = k_ref[rows, cols]
            vb = v_ref[rows, cols].astype(BF16)
            kb = k.astype(BF16)
            s = s_ref[p]
            sc_a = (_dot_nt(jnp.where(first, q, 0.0).astype(BF16), kb) * dec_ref[2 * p]).astype(BF16)
            sc_b = (_dot_nt(jnp.where(first, 0.0, q).astype(BF16), kb) * dec_ref[2 * p + 1]).astype(BF16)
            o = jnp.where(first, _dot(sc_a, vb), _dot(sc_b, vb)) + _dot(q.astype(BF16), s.astype(BF16)) * rw_ref[p]
            kw_t = (k * ww_ref[p]).T.astype(BF16)
            s_ref[p] = s * cd_ref[p] + jnp.where(same_head, _dot(kw_t, vb), 0.0)
            d = o - half_mean(o)
            o_ref[rows, cols] = d * lax.rsqrt(half_mean(d * d) + EPS) * g_ref[p]
        return carry

    lax.fori_loop(0, n_chunks, chunk, 0)

    @pl.when(r == pl.num_programs(1) - 1)
    def _():
        for h in range(N_RET_HEADS):
            lo = (h % 2) * HEAD_DIM
            sfin_ref[0, h] = s_ref[h // 2][lo:lo + HEAD_DIM, lo:lo + HEAD_DIM]


def _retention_tables(length):
    log_g = jnp.log1p(-jnp.exp2(-5.0 - jnp.arange(N_RET_HEADS, dtype=F32)))
    n = jnp.arange(length, dtype=F32)
    diff = n[:, None] - n[None, :]
    dec = jnp.where((diff >= 0)[None], jnp.exp(log_g[:, None, None] * jnp.maximum(diff, 0.0)[None]), 0.0)
    read_w = jnp.exp(log_g[:, None] * (n + 1.0)[None])
    write_w = jnp.exp(log_g[:, None] * (length - 1.0 - n)[None])
    chunk_decay = jnp.exp(log_g * length)
    return dec, read_w, write_w, chunk_decay


def _retention_prompt(z, ret_g, batch, seq, *, rows_per_step):
    m = z.shape[0]
    nr = seq // rows_per_step
    dec, read_w, write_w, chunk_decay = _retention_tables(RET_CHUNK)
    n_pairs = N_RET_HEADS // 2

    def per_lane(t):
        t = jnp.broadcast_to(t[:, :, None], (N_RET_HEADS, t.shape[1], HEAD_DIM))
        return t.reshape(n_pairs, 2, t.shape[1], HEAD_DIM).transpose(0, 2, 1, 3).reshape(n_pairs, t.shape[1], LANES)

    rw = per_lane(read_w)
    ww = per_lane(write_w)
    cd = jnp.repeat(per_lane(jnp.broadcast_to(chunk_decay[:, None], (N_RET_HEADS, HEAD_DIM))), 2, axis=1)
    full3 = lambda shape: pl.BlockSpec(shape, lambda b, r: (0, 0, 0))
    sec = lambda c: pl.BlockSpec((rows_per_step, SECTION), lambda b, r: (b * nr + r, c))
    return pl.pallas_call(
        functools.partial(_ret_prompt_kernel, n_chunks=rows_per_step // RET_CHUNK),
        grid=(batch, nr),
        in_specs=[sec(0), sec(1), sec(2),
                  full3((N_RET_HEADS, RET_CHUNK, RET_CHUNK)),
                  full3((n_pairs, RET_CHUNK, LANES)),
                  full3((n_pairs, RET_CHUNK, LANES)),
                  full3((n_pairs, LANES, LANES)),
                  full3((n_pairs, 1, LANES))],
        out_specs=[pl.BlockSpec((rows_per_step, SECTION), lambda b, r: (b * nr + r, 0)),
                   pl.BlockSpec((1, N_RET_HEADS, HEAD_DIM, HEAD_DIM), lambda b, r: (b, 0, 0, 0))],
        out_shape=[jax.ShapeDtypeStruct((m, SECTION), F32),
                   jax.ShapeDtypeStruct((batch, N_RET_HEADS, HEAD_DIM, HEAD_DIM), F32)],
        scratch_shapes=[pltpu.VMEM((n_pairs, LANES, LANES), F32)],
        compiler_params=_cparams(("parallel", "arbitrary")),
        name="retention_prompt",
    )(z, z, z, dec, rw, ww, cd, ret_g.reshape(n_pairs, 1, LANES))


def _ret_sample_kernel(q_ref, k_ref, v_ref, s_ref, dec_ref, rw_ref, ww_ref, cd_ref, g_ref, *rest):
    o_ref, snew_ref = rest[-2:]
    n = q_ref.shape[0] * q_ref.shape[1]
    length = q_ref.shape[2]
    q = q_ref[...].reshape(n, length, HEAD_DIM)
    k = k_ref[...].reshape(n, length, HEAD_DIM)
    vb = v_ref[...].reshape(n, length, HEAD_DIM).astype(BF16)
    s = s_ref[...].reshape(n, HEAD_DIM, HEAD_DIM)
    qb = q.astype(BF16)
    scores = jnp.einsum('nqd,nkd->nqk', qb, k.astype(BF16), preferred_element_type=F32) * dec_ref[...]
    o = jnp.einsum('nqk,nke->nqe', scores.astype(BF16), vb, preferred_element_type=F32)
    o = o + jnp.einsum('nqd,nde->nqe', qb, s.astype(BF16), preferred_element_type=F32) * rw_ref[...]
    kw = (k * ww_ref[...]).astype(BF16)
    s_new = s * cd_ref[...] + jnp.einsum('nld,nle->nde', kw, vb, preferred_element_type=F32)
    mu = jnp.mean(o, axis=-1, keepdims=True)
    var = jnp.mean(jnp.square(o - mu), axis=-1, keepdims=True)
    o_ref[...] = ((o - mu) * lax.rsqrt(var + EPS) * g_ref[...]).reshape(o_ref.shape)
    snew_ref[...] = s_new.reshape(snew_ref.shape)


def _retention_sample(q, k, v, state, layer, depth, new_states, ret_g, n_tokens, *, bb):
    db, nh, length, _ = q.shape
    prior = () if new_states is None else (new_states,)
    dec, read_w, write_w, chunk_decay = _retention_tables(n_tokens)
    pad = length - n_tokens
    dec = jnp.pad(dec, ((0, 0), (0, pad), (0, pad)))
    read_w = jnp.pad(read_w, ((0, 0), (0, pad)))
    write_w = jnp.pad(write_w, ((0, 0), (0, pad)))
    n = bb * nh
    tile = lambda t: jnp.tile(t, (bb,) + (1,) * (t.ndim - 1))
    dec_t = tile(dec)
    rw_t = tile(jnp.broadcast_to(read_w[:, :, None], (nh, length, HEAD_DIM)))
    ww_t = tile(jnp.broadcast_to(write_w[:, :, None], (nh, length, HEAD_DIM)))
    cd_t = tile(jnp.broadcast_to(chunk_decay[:, None, None], (nh, HEAD_DIM, HEAD_DIM)))
    g_t = tile(jnp.broadcast_to(ret_g[:, None, :], (nh, length, HEAD_DIM)))
    blk = pl.BlockSpec((bb, nh, length, HEAD_DIM), lambda i: (i, 0, 0, 0))
    sblk = pl.BlockSpec((bb, nh, HEAD_DIM, HEAD_DIM), lambda i: (i, 0, 0, 0))
    sout = pl.BlockSpec((None, bb, nh, HEAD_DIM, HEAD_DIM), lambda i: (layer, i, 0, 0, 0))
    full = lambda a: pl.BlockSpec(a.shape, lambda i: (0,) * a.ndim)
    return pl.pallas_call(
        _ret_sample_kernel,
        grid=(db // bb,),
        in_specs=[blk, blk, blk, sblk, full(dec_t), full(rw_t), full(ww_t), full(cd_t), full(g_t)]
        + [pl.BlockSpec(memory_space=pl.ANY)] * len(prior),
        out_specs=[blk, sout],
        out_shape=[jax.ShapeDtypeStruct(q.shape, F32), jax.ShapeDtypeStruct((depth,) + state.shape, F32)],
        input_output_aliases={9: 1} if prior else {},
        compiler_params=_cparams(("parallel",)),
        name="retention_sample",
    )(q, k, v, state, dec_t, rw_t, ww_t, cd_t, g_t, *prior)


def _lambda(lq1_ref, lk1_ref, lq2_ref, lk2_ref, lam_init):
    e1 = jnp.exp(jnp.sum(lq1_ref[...] * lk1_ref[...], axis=-1, keepdims=True))
    e2 = jnp.exp(jnp.sum(lq2_ref[...] * lk2_ref[...], axis=-1, keepdims=True))
    return e1 - e2 + lam_init


def _diff_prompt_kernel(qi_ref, ki_ref, q_ref, k_ref, v_ref, lq1_ref, lk1_ref, lq2_ref, lk2_ref, g_ref, *rest,
                        lam_init, heads):
    o_ref, ko_ref, vo_ref, m_ref, acc_ref, qm_ref = rest[-6:]
    p = pl.program_id(2)
    qi = qi_ref[p]
    ki = ki_ref[p]
    tk = k_ref.shape[0]

    @pl.when(ki == 0)
    def _():
        m_ref[...] = jnp.full_like(m_ref, NEG_BIG)
        acc_ref[...] = jnp.zeros_like(acc_ref)
        lane = lax.broadcasted_iota(jnp.int32, (q_ref.shape[0], DIFF_DV), 1)
        for h in range(heads):
            q = q_ref[:, h * DIFF_DV:(h + 1) * DIFF_DV] * (HEAD_DIM ** -0.5 * LOG2_E)
            qm_ref[2 * h] = jnp.where(lane < HEAD_DIM, q, 0.0).astype(BF16)
            qm_ref[2 * h + 1] = jnp.where(lane < HEAD_DIM, 0.0, q).astype(BF16)

    def step(masked):
        ones = jnp.ones((ONES_ROWS, tk), BF16)
        kb = [None] * heads

        def scores(idx):
            h = idx // 2
            if kb[h] is None:
                kb[h] = k_ref[:, h * DIFF_DV:(h + 1) * DIFF_DV].astype(BF16)
            return _dot_nt(kb[h], qm_ref[idx])

        def accumulate(idx, s, v_ext):
            if masked:
                key = lax.broadcasted_iota(jnp.int32, s.shape, 0)
                qry = lax.broadcasted_iota(jnp.int32, s.shape, 1)
                s = jnp.where(key <= qry, s, NEG_BIG)
            m_old = m_ref[idx]
            m_new = jnp.maximum(m_old, jnp.max(s, axis=0, keepdims=True))
            alpha = jnp.exp2(m_old - m_new)
            e = jnp.exp2(s - m_new).astype(BF16)
            acc_ref[idx] = alpha * acc_ref[idx] + _dot(v_ext, e)
            m_ref[idx] = m_new

        n = 2 * heads
        s_next = scores(0)
        v_ext = None
        for idx in range(n):
            s_cur = s_next
            if idx + 1 < n:
                s_next = scores(idx + 1)
            if idx % 2 == 0:
                hcols = slice((idx // 2) * DIFF_DV, (idx // 2 + 1) * DIFF_DV)
                v_ext = jnp.concatenate([v_ref[:, hcols].T.astype(BF16), ones], axis=0)
            accumulate(idx, s_cur, v_ext)

    @pl.when(ki < qi)
    def _():
        step(False)

    @pl.when(ki == qi)
    def _():
        step(True)
        for h in range(heads):
            ko_ref[:, h, :] = k_ref[:, h * DIFF_DV:(h + 1) * DIFF_DV]
            vo_ref[:, h, :] = v_ref[:, h * DIFF_DV:(h + 1) * DIFF_DV]
        lam = _lambda(lq1_ref, lk1_ref, lq2_ref, lk2_ref, lam_init)
        for h in range(heads):
            a1 = acc_ref[2 * h]
            a2 = acc_ref[2 * h + 1]
            o_t = (a1[:DIFF_DV] / a1[DIFF_DV:DIFF_DV + 1] - lam * (a2[:DIFF_DV] / a2[DIFF_DV:DIFF_DV + 1]))
            o_ref[:, h * DIFF_DV:(h + 1) * DIFF_DV] = _rms(o_t.T, g_ref[h]) * (1.0 - lam_init)


def _diff_prompt(z, dk, dv, lam_params, diff_g, lam_init, batch, seq, layer, depth, stacks, *, tile, heads):
    m = z.shape[0]
    nt = seq // tile
    width = heads * DIFF_DV
    groups = N_DIFF_HEADS // heads
    assert groups == 1
    pairs = [(qi, ki) for qi in range(nt) for ki in range(qi + 1)]
    qi_tab = jnp.asarray([p[0] for p in pairs], jnp.int32)
    ki_tab = jnp.asarray([p[1] for p in pairs], jnp.int32)
    q_col0 = 4 * SECTION // width
    stacks = () if stacks is None else tuple(stacks)
    lam_spec = pl.BlockSpec((1, HEAD_DIM), lambda b, h, p, qi, ki: (0, 0))
    stack_spec = pl.BlockSpec((None, tile, N_DIFF_HEADS, DIFF_DV),
                              lambda b, h, p, qi, ki: (layer, b * nt + qi[p], 0, 0))
    grid_spec = pltpu.PrefetchScalarGridSpec(
        num_scalar_prefetch=2,
        grid=(batch, groups, len(pairs)),
        in_specs=[pl.BlockSpec((tile, width), lambda b, h, p, qi, ki: (b * nt + qi[p], q_col0 + h)),
                  pl.BlockSpec((tile, width), lambda b, h, p, qi, ki: (b * nt + ki[p], h)),
                  pl.BlockSpec((tile, width), lambda b, h, p, qi, ki: (b * nt + ki[p], h)),
                  lam_spec, lam_spec, lam_spec, lam_spec,
                  pl.BlockSpec((heads, 1, DIFF_DV), lambda b, h, p, qi, ki: (h, 0, 0))]
        + [pl.BlockSpec(memory_space=pl.ANY)] * len(stacks),
        out_specs=[pl.BlockSpec((tile, width), lambda b, h, p, qi, ki: (b * nt + qi[p], h)), stack_spec, stack_spec],
        scratch_shapes=[pltpu.VMEM((2 * heads, 1, tile), F32),
                        pltpu.VMEM((2 * heads, DIFF_DV + ONES_ROWS, tile), F32),
                        pltpu.VMEM((2 * heads, tile, DIFF_DV), BF16)],
    )
    stack_shape = jax.ShapeDtypeStruct((depth, m, N_DIFF_HEADS, DIFF_DV), F32)
    n_in = 2 + 3 + len(lam_params) + 1
    return pl.pallas_call(
        functools.partial(_diff_prompt_kernel, lam_init=lam_init, heads=heads),
        grid_spec=grid_spec,
        out_shape=[jax.ShapeDtypeStruct((m, N_DIFF_HEADS * DIFF_DV), F32), stack_shape, stack_shape],
        input_output_aliases={n_in + s: 1 + s for s in range(len(stacks))},
        compiler_params=_cparams(("parallel", "parallel", "arbitrary")),
        name="diff_attn_prompt",
    )(qi_tab, ki_tab, z, dk, dv, *lam_params, diff_g.reshape(N_DIFF_HEADS, 1, DIFF_DV), *stacks)


def _diff_sample_kernel(pt_ref, q_ref, kn_ref, vn_ref, *rest, n_pages, lam_init):
    k_pages = rest[:n_pages]
    v_pages = rest[n_pages:2 * n_pages]
    g_ref, lq1_ref, lk1_ref, lq2_ref, lk2_ref, o_ref, s_ref = rest[2 * n_pages:]
    half = q_ref.shape[1]
    n_rows = 2 * half
    page_rows = k_pages[0].shape[0]
    qf = q_ref[0] * (HEAD_DIM ** -0.5)
    lane = lax.broadcasted_iota(jnp.int32, qf.shape, 1)
    q = jnp.concatenate([jnp.where(lane < HEAD_DIM, qf, 0.0), jnp.where(lane < HEAD_DIM, 0.0, qf)],
                        axis=0).astype(BF16)

    row_h = lax.broadcasted_iota(jnp.int32, (n_rows, page_rows), 0) % N_DIFF_HEADS
    col_h = lax.broadcasted_iota(jnp.int32, (n_rows, page_rows), 1) % N_DIFF_HEADS
    same_head = row_h == col_h
    m = jnp.full((n_rows, 1), NEG_BIG, F32)
    for p in range(n_pages):
        s = jnp.where(same_head, _dot_nt(q, k_pages[p][...].astype(BF16)), NEG_BIG)
        s_ref[:, p * page_rows:(p + 1) * page_rows] = s
        m = jnp.maximum(m, jnp.max(s, axis=-1, keepdims=True))

    n_new = kn_ref.shape[1]
    r = lax.broadcasted_iota(jnp.int32, (n_rows, n_new), 0)
    c = lax.broadcasted_iota(jnp.int32, (n_rows, n_new), 1)
    ok = (r % N_DIFF_HEADS == c % N_DIFF_HEADS) & (c // N_DIFF_HEADS <= (r % half) // N_DIFF_HEADS)
    s_new = jnp.where(ok, _dot_nt(q, kn_ref[0].astype(BF16)), NEG_BIG)
    m = jnp.maximum(m, jnp.max(s_new, axis=-1, keepdims=True))

    e_new = jnp.exp(s_new - m)
    l = jnp.sum(e_new, axis=-1, keepdims=True)
    for p in range(n_pages):
        cols = slice(p * page_rows, (p + 1) * page_rows)
        e = jnp.exp(s_ref[:, cols] - m)
        s_ref[:, cols] = e
        l = l + jnp.sum(e, axis=-1, keepdims=True)

    lam = _lambda(lq1_ref, lk1_ref, lq2_ref, lk2_ref, lam_init)
    inv = 1.0 / l
    w1 = inv[:half]
    w2 = lam * inv[half:]
    a_new = (e_new[:half] * w1 - e_new[half:] * w2).astype(BF16)
    o = _dot(a_new, vn_ref[0].astype(BF16))
    for p in range(n_pages):
        cols = slice(p * page_rows, (p + 1) * page_rows)
        a = (s_ref[:half, cols] * w1 - s_ref[half:, cols] * w2).astype(BF16)
        o = o + _dot(a, v_pages[p][...].astype(BF16))
    o_ref[0] = _rms(o, g_ref[...]) * (1.0 - lam_init)


def _diff_sample(q_new, k_new, v_new, k_cache, v_cache, page_table, layer, lam_params, g_rows, lam_init):
    db = q_new.shape[0]
    n_rows = 2 * q_new.shape[1]
    n_pages = page_table.shape[1]
    page_rows = k_cache.shape[2]
    n_new = k_new.shape[1]

    def page_spec(p):
        return pl.BlockSpec((None, None, page_rows, LANES), lambda b, pt: (layer, pt[b * n_pages + p], 0, 0))

    small = lambda rows: pl.BlockSpec((1, rows, LANES), lambda b, pt: (b, 0, 0))
    lam_spec = pl.BlockSpec((1, HEAD_DIM), lambda b, pt: (0, 0))
    grid_spec = pltpu.PrefetchScalarGridSpec(
        num_scalar_prefetch=1,
        grid=(db,),
        in_specs=[small(n_new), small(n_new), small(n_new)]
        + [page_spec(p) for p in range(n_pages)] * 2
        + [pl.BlockSpec((n_rows // 2, LANES), lambda b, pt: (0, 0)), lam_spec, lam_spec, lam_spec, lam_spec],
        out_specs=small(n_rows // 2),
        scratch_shapes=[pltpu.VMEM((n_rows, n_pages * page_rows), F32)],
    )
    return pl.pallas_call(
        functools.partial(_diff_sample_kernel, n_pages=n_pages, lam_init=lam_init),
        grid_spec=grid_spec,
        out_shape=jax.ShapeDtypeStruct((db, n_rows // 2, LANES), F32),
        compiler_params=_cparams(("parallel",)),
        name="diff_attn_sample",
    )(page_table.reshape(-1), q_new, k_new, v_new, *([k_cache] * n_pages), *([v_cache] * n_pages), g_rows,
      *lam_params)


def _merge_kernel(x_ref, rg_ref, ro_ref, do_ref, w_ref, g_ref, o_ref):
    a = jnp.concatenate([_silu(rg_ref[...]) * ro_ref[...], do_ref[...]], axis=-1).astype(BF16)
    o_ref[...] = x_ref[...] + _rms(_dot(a, w_ref[...]), g_ref[...])


def _merge(x, z, ro, do, w_bf, layer, g, *, tm):
    m, d = x.shape
    rg_col = 3
    row = lambda w: pl.BlockSpec((tm, w), lambda i: (i, 0))
    return pl.pallas_call(
        _merge_kernel,
        grid=(m // tm,),
        in_specs=[row(d), pl.BlockSpec((tm, SECTION), lambda i: (i, rg_col)), row(SECTION), row(SECTION),
                  _layer_spec(w_bf, layer), pl.BlockSpec((1, d), lambda i: (0, 0))],
        out_specs=row(d),
        out_shape=jax.ShapeDtypeStruct((m, d), F32),
        compiler_params=_cparams(("parallel",)),
        name="mixer_merge",
    )(x, z, ro, do, w_bf, g.reshape(1, d))


def _mem_prompt_kernel(x_ref, gpre_ref, wq_ref, mk_ref, mv_ref, wo_ref, gpost_ref, o_ref):
    x = x_ref[...]
    q = _dot(_rms(x, gpre_ref[...]).astype(BF16), wq_ref[...])
    hd = q.shape[1] // N_MEM_HEADS

    def scores(h):
        cols = slice(h * hd, (h + 1) * hd)
        return _dot_nt(q[:, cols].astype(BF16), mk_ref[0, :, cols].astype(BF16)) * (hd ** -0.5)

    outs = []
    s_next = scores(0)
    for h in range(N_MEM_HEADS):
        s = s_next
        if h + 1 < N_MEM_HEADS:
            s_next = scores(h + 1)
        e = jnp.exp(s - jnp.max(s, axis=-1, keepdims=True))
        p = e / jnp.sum(e, axis=-1, keepdims=True)
        outs.append(_dot(p.astype(BF16), mv_ref[0, :, h * hd:(h + 1) * hd].astype(BF16)))
    o = jnp.concatenate(outs, axis=-1).astype(BF16)
    o_ref[...] = x + _rms(_dot(o, wo_ref[...]), gpost_ref[...])


def _mem_prompt(x, g_pre, wq_bf, mk, mv, wo_bf, layer, g_post, seq, *, tm):
    m, d = x.shape
    n_mem = mk.shape[1]
    blocks_per_seq = seq // tm
    vec = pl.BlockSpec((1, d), lambda i: (0, 0))
    mem = pl.BlockSpec((1, n_mem, d), lambda i: (i // blocks_per_seq, 0, 0))
    row = pl.BlockSpec((tm, d), lambda i: (i, 0))
    return pl.pallas_call(
        _mem_prompt_kernel,
        grid=(m // tm,),
        in_specs=[row, vec, _layer_spec(wq_bf, layer), mem, mem, _layer_spec(wo_bf, layer), vec],
        out_specs=row,
        out_shape=jax.ShapeDtypeStruct((m, d), F32),
        compiler_params=_cparams(("parallel",)),
        name="mem_attn_prompt",
    )(x, g_pre.reshape(1, d), wq_bf, mk, mv, wo_bf, g_post.reshape(1, d))


def _mem_sample_kernel(q_ref, k_ref, v_ref, o_ref):
    bb, n_rows, _ = q_ref.shape
    half = n_rows // 2
    key_rows = k_ref.shape[1]
    group = 2 * N_MEM_HEADS
    row = lax.broadcasted_iota(jnp.int32, (half, key_rows), 0)
    col = lax.broadcasted_iota(jnp.int32, (half, key_rows), 1)
    own = (col % group) == (row % N_MEM_HEADS)
    scale = (2 * LANES) ** -0.5
    for b in range(bb):
        sp = _dot_nt(q_ref[b].astype(BF16), k_ref[b].astype(BF16))
        s = (sp[:half] + pltpu.roll(sp[half:], key_rows - N_MEM_HEADS, 1)) * scale
        s = jnp.where(own, s, NEG_BIG)
        e = jnp.exp(s - jnp.max(s, axis=-1, keepdims=True))
        p = e / jnp.sum(e, axis=-1, keepdims=True)
        p2 = jnp.concatenate([p, pltpu.roll(p, N_MEM_HEADS, 1)], axis=0).astype(BF16)
        o = _dot(p2, v_ref[b].astype(BF16))
        o_ref[b] = jnp.concatenate([o[:half], o[half:]], axis=-1)


def _mem_sample(q_rows, k_cache, v_cache, layer, *, bb):
    db, n_rows, _ = q_rows.shape
    key_rows = k_cache.shape[2]
    qspec = pl.BlockSpec((bb, n_rows, LANES), lambda i: (i, 0, 0))
    cspec = pl.BlockSpec((None, bb, key_rows, LANES), lambda i: (layer, i, 0, 0))
    return pl.pallas_call(
        _mem_sample_kernel,
        grid=(db // bb,),
        in_specs=[qspec, cspec, cspec],
        out_specs=pl.BlockSpec((bb, n_rows // 2, 2 * LANES), lambda i: (i, 0, 0)),
        out_shape=jax.ShapeDtypeStruct((db, n_rows // 2, 2 * LANES), F32),
        compiler_params=_cparams(("parallel",)),
        name="mem_attn_sample",
    )(q_rows, k_cache, v_cache)


def _out_proj_kernel(x_ref, a_ref, w_ref, g_ref, o_ref):
    o_ref[...] = x_ref[...] + _rms(_dot(a_ref[...].astype(BF16), w_ref[...]), g_ref[...])


def _out_proj(x, a, w_bf, layer, g, *, tm):
    m, d = x.shape
    row = pl.BlockSpec((tm, d), lambda i: (i, 0))
    return pl.pallas_call(
        _out_proj_kernel,
        grid=(m // tm,),
        in_specs=[row, pl.BlockSpec((tm, a.shape[1]), lambda i: (i, 0)), _layer_spec(w_bf, layer),
                  pl.BlockSpec((1, d), lambda i: (0, 0))],
        out_specs=row,
        out_shape=jax.ShapeDtypeStruct((m, d), F32),
        compiler_params=_cparams(("parallel",)),
        name="out_proj",
    )(x, a, w_bf, g.reshape(1, d))


def _ffn_kernel(x_ref, gpre_ref, wug_ref, wuv_ref, cwg_ref, cwv_ref, cbg_ref, cbv_ref, wd_ref, hg_ref, hv_ref,
                gpost_ref, o_ref, tg_ref, tv_ref, xn_ref, acc_ref, halo_g_ref, halo_v_ref, *, shift,
                blocks_per_seq):
    i = pl.program_id(0)
    j = pl.program_id(1)
    tm = x_ref.shape[0]

    @pl.when(j == 0)
    def _():
        xn_ref[...] = _rms(x_ref[...], gpre_ref[...]).astype(BF16)
        acc_ref[...] = jnp.zeros_like(acc_ref)

    seq_start = (i % blocks_per_seq) == 0

    def conv(w_ref, cw_ref, cb_ref, hist_ref, halo_ref, tail_ref):
        u = _dot(xn_ref[...], w_ref[...])
        prev = jnp.where(seq_start, hist_ref[0], halo_ref[j])
        ue = jnp.concatenate([prev, u], axis=0)
        u2 = ue[:tm]
        u1 = ue[shift:shift + tm]
        tail = u[tm - 2 * shift:]
        halo_ref[j] = tail
        tail_ref[0] = tail
        cw = cw_ref[...]
        return cb_ref[...] + cw[0:1] * u2 + cw[1:2] * u1 + cw[2:3] * u

    gate = conv(wug_ref, cwg_ref, cbg_ref, hg_ref, halo_g_ref, tg_ref)
    val = conv(wuv_ref, cwv_ref, cbv_ref, hv_ref, halo_v_ref, tv_ref)
    acc_ref[...] += _dot((_silu(gate) * val).astype(BF16), wd_ref[...])

    @pl.when(j == pl.num_programs(1) - 1)
    def _():
        o_ref[...] = x_ref[...] + _rms(acc_ref[...], gpost_ref[...])


def _ffn_rows_kernel(x_ref, gpre_ref, wu_ref, cw_ref, cb_ref, wd_ref, hist_ref, gpost_ref, o_ref, tail_ref,
                     xn_ref, h_ref, halo_ref, *, tf, blocks_per_seq):
    i = pl.program_id(0)
    tm = x_ref.shape[0]
    f = wd_ref.shape[0]
    n_tiles = f // tf
    xn_ref[...] = _rms(x_ref[...], gpre_ref[...]).astype(BF16)
    seq_start = (i % blocks_per_seq) == 0
    rows = lax.broadcasted_iota(jnp.int32, (tm, tf), 0)

    def up(col):
        return _dot(xn_ref[...], wu_ref[:, col:col + tf])

    def conv(u, col):
        cols = slice(col, col + tf)
        prev = jnp.where(seq_start, hist_ref[0, :, cols], halo_ref[:, cols])
        u1 = jnp.where(rows == 0, prev[1:2], pltpu.roll(u, 1, 0))
        u2 = jnp.where(rows == 0, prev[0:1], jnp.where(rows == 1, prev[1:2], pltpu.roll(u, 2, 0)))
        tail = u[tm - 2:]
        halo_ref[:, cols] = tail
        tail_ref[0, :, cols] = tail
        cw = cw_ref[:, cols]
        return cb_ref[:, cols] + cw[0:1] * u2 + cw[1:2] * u1 + cw[2:3] * u

    ug, uv = up(0), up(f)
    for t in range(n_tiles):
        cg, cv = ug, uv
        if t + 1 < n_tiles:
            ug, uv = up((t + 1) * tf), up(f + (t + 1) * tf)
        gate = conv(cg, t * tf)
        val = conv(cv, f + t * tf)
        h_ref[:, t * tf:(t + 1) * tf] = (_silu(gate) * val).astype(BF16)
    o_ref[...] = x_ref[...] + _rms(_dot(h_ref[...], wd_ref[...]), gpost_ref[...])


def _conv_ffn_rows(x, g_pre, wu_bf, conv_w, conv_b, wd_bf, layer, hist, g_post, *, tm, tf, blocks_per_seq):
    m, d = x.shape
    f = wd_bf.shape[1]
    nblk = m // tm
    halo = CONV_WIDTH - 1
    const = lambda a: pl.BlockSpec(a.shape, lambda i: (0,) * a.ndim, pipeline_mode=pl.Buffered(1))
    row = pl.BlockSpec((tm, d), lambda i: (i, 0))
    g_pre, g_post, conv_b = g_pre.reshape(1, d), g_post.reshape(1, d), conv_b.reshape(1, 2 * f)
    return pl.pallas_call(
        functools.partial(_ffn_rows_kernel, tf=tf, blocks_per_seq=blocks_per_seq),
        grid=(nblk,),
        in_specs=[row, const(g_pre), _layer_spec(wu_bf, layer), const(conv_w), const(conv_b),
                  _layer_spec(wd_bf, layer),
                  pl.BlockSpec((1, halo, 2 * f), lambda i: (i // blocks_per_seq, 0, 0)), const(g_post)],
        out_specs=[row, pl.BlockSpec((1, halo, 2 * f), lambda i: (i, 0, 0))],
        out_shape=[jax.ShapeDtypeStruct((m, d), F32), jax.ShapeDtypeStruct((nblk, halo, 2 * f), F32)],
        scratch_shapes=[pltpu.VMEM((tm, d), BF16), pltpu.VMEM((tm, f), BF16), pltpu.VMEM((halo, 2 * f), F32)],
        compiler_params=_cparams(("arbitrary",)),
        name="conv_ffn_rows",
    )(x, g_pre, wu_bf, conv_w, conv_b, wd_bf, hist, g_post)


def _conv_ffn(x, g_pre, wu_bf, conv_w, conv_b, wd_bf, layer, hist, g_post, *, tm, tf, shift, blocks_per_seq):
    m, d = x.shape
    assert shift % 8 == 0 and tm >= 2 * shift
    f = wd_bf.shape[1]
    nj = f // tf
    nblk = m // tm
    halo = 2 * shift
    up = lambda off: pl.BlockSpec((None, d, tf), lambda i, j: (layer, 0, off + j))
    cw = lambda off: pl.BlockSpec((CONV_WIDTH, tf), lambda i, j: (0, off + j))
    cb = lambda off: pl.BlockSpec((1, tf), lambda i, j: (0, off + j))
    hs = lambda off: pl.BlockSpec((1, halo, tf), lambda i, j: (i // blocks_per_seq, 0, off + j))
    vec = pl.BlockSpec((1, d), lambda i, j: (0, 0))
    row = pl.BlockSpec((tm, d), lambda i, j: (i, 0))
    tail = pl.BlockSpec((1, halo, tf), lambda i, j: (i, 0, j))
    return pl.pallas_call(
        functools.partial(_ffn_kernel, shift=shift, blocks_per_seq=blocks_per_seq),
        grid=(nblk, nj),
        in_specs=[row, vec, up(0), up(nj), cw(0), cw(nj), cb(0), cb(nj),
                  pl.BlockSpec((None, tf, d), lambda i, j: (layer, j, 0)), hs(0), hs(nj), vec],
        out_specs=[row, tail, tail],
        out_shape=[jax.ShapeDtypeStruct((m, d), F32),
                   jax.ShapeDtypeStruct((nblk, halo, f), F32),
                   jax.ShapeDtypeStruct((nblk, halo, f), F32)],
        scratch_shapes=[pltpu.VMEM((tm, d), BF16), pltpu.VMEM((tm, d), F32),
                        pltpu.VMEM((nj, halo, tf), F32), pltpu.VMEM((nj, halo, tf), F32)],
        compiler_params=_cparams(("arbitrary", "arbitrary")),
        name="conv_ffn",
    )(x, g_pre.reshape(1, d), wu_bf, wu_bf, conv_w, conv_w, conv_b.reshape(1, -1), conv_b.reshape(1, -1), wd_bf,
      hist, hist, g_post.reshape(1, d))


def _rope_tables(pos):
    inv = 1.0 / (ROPE_THETA ** (jnp.arange(0, HEAD_DIM, 2, dtype=F32) / HEAD_DIM))
    ang = pos.astype(F32)[:, None] * inv[None, :]
    ang = jnp.concatenate([ang, ang], axis=-1)
    cos, sin = jnp.cos(ang), jnp.sin(ang)
    first_half = jnp.arange(HEAD_DIM) < HEAD_DIM // 2
    sin_a = jnp.where(first_half[None], -sin, 0.0)
    sin_b = jnp.where(first_half[None], 0.0, sin)
    rep = LANES // HEAD_DIM
    return tuple(jnp.tile(t, (1, rep)) for t in (cos, sin_a, sin_b))


def _tile_for(n, target):
    t = min(n, target)
    assert n % t == 0, (n, t)
    return t


def kernel(x_prompt, x_sample, mem_prompt, state_ret, cache_diff_k, cache_diff_v, page_table, cache_mem_k, cache_mem_v, state_conv, g_mix_pre, g_mix_post, w_in, ret_norm_g, lam_q1, lam_k1, lam_q2, lam_k2, diff_norm_g, w_out, g_mem_pre, g_mem_post, g_mem_kv, w_mq, w_mk, w_mv, w_mo, g_ffn_pre, g_ffn_post, w_up, conv_w, conv_b, w_down):
    B, S, D = x_prompt.shape
    DB, DS, _ = x_sample.shape
    depth = w_in.shape[0]
    n_pool, page = cache_diff_k.shape[1], cache_diff_k.shape[2]
    past = page_table.shape[1] * page
    n_mem = mem_prompt.shape[1]
    two_f = w_up.shape[2]
    mem_hd = D // N_MEM_HEADS
    mp, ms = B * S, DB * DS

    tm_p = _tile_for(S, 512)
    tile_attn = _tile_for(S, 512)
    tf = 256
    assert (two_f // 2) % tf == 0
    assert mem_hd == 2 * LANES

    tabs_p = _rope_tables(jnp.arange(S, dtype=jnp.int32))
    tabs_s = _rope_tables(jnp.tile(past + jnp.arange(DS, dtype=jnp.int32), DB))
    no_tabs_mem = tuple(jnp.zeros((B * n_mem, LANES), F32) for _ in range(3))
    no_tabs_s = tuple(jnp.zeros((ms, LANES), F32) for _ in range(3))

    kc = cache_diff_k.reshape(depth, n_pool, page * N_DIFF_HEADS, DIFF_DV)
    vc = cache_diff_v.reshape(depth, n_pool, page * N_DIFF_HEADS, DIFF_DV)

    def mem_rows(c):
        c = c.reshape(depth, DB, n_mem, N_MEM_HEADS, 2, LANES).transpose(0, 1, 2, 4, 3, 5)
        return c.reshape(depth, DB, n_mem * 2 * N_MEM_HEADS, LANES)

    mkc = mem_rows(cache_mem_k)
    mvc = mem_rows(cache_mem_v)

    xp = x_prompt.reshape(mp, D)
    xs = x_sample.reshape(ms, D)
    mem2d = mem_prompt.reshape(B * n_mem, D)
    zero_hist = jnp.zeros((B, 2, two_f), F32)
    ret_pad = 8

    outs = {k: [] for k in ("p_ret", "p_mk", "p_mv", "p_conv", "s_dk", "s_dv", "s_conv")}
    kv_stacks = None
    s_ret = None
    rope_secs = (0, 1, 4, 5)
    w_in_bf = w_in.astype(BF16)
    w_out_bf = w_out.astype(BF16)
    w_mq_bf = w_mq.astype(BF16)
    w_mo_bf = w_mo.astype(BF16)
    w_mkv_bf = jnp.concatenate([w_mk, w_mv], axis=2).astype(BF16)
    w_up_bf = w_up.astype(BF16)
    w_down_bf = w_down.astype(BF16)
    for l in range(depth):
        lam_init = 0.8 - 0.6 * math.exp(-0.3 * l)
        lam_params = [t[l].reshape(1, HEAD_DIM) for t in (lam_q1, lam_k1, lam_q2, lam_k2)]

        z, dk, dv = _project(xp, g_mix_pre[l], w_in_bf, l, tabs_p, tm=tm_p, tn=SECTION, rope_sections=rope_secs,
                             k_section=1, n_main=5)
        ro, s_fin = _retention_prompt(z, ret_norm_g[l], B, S, rows_per_step=tm_p)
        do, *kv_stacks = _diff_prompt(z, dk, dv, lam_params, diff_norm_g[l], lam_init, B, S, l, depth, kv_stacks,
                                      tile=tile_attn, heads=N_DIFF_HEADS)
        xp = _merge(xp, z, ro, do, w_out_bf, l, g_mix_post[l], tm=tm_p)
        outs["p_ret"].append(s_fin)

        zs, dqs, dks, dvs = _project(xs, g_mix_pre[l], w_in_bf, l, tabs_s, tm=ms, tn=SECTION,
                                     rope_sections=rope_secs, k_section=1, n_main=4)

        def heads_first(t):
            t = t.reshape(DB, DS, N_RET_HEADS, HEAD_DIM).transpose(0, 2, 1, 3)
            return jnp.pad(t, ((0, 0), (0, 0), (0, ret_pad - DS), (0, 0)))

        ro_s, s_ret = _retention_sample(heads_first(zs[:, :SECTION]), heads_first(zs[:, SECTION:2 * SECTION]),
                                        heads_first(zs[:, 2 * SECTION:3 * SECTION]), state_ret[l], l, depth, s_ret,
                                        ret_norm_g[l],
                                        DS, bb=8)
        ro_s = ro_s[:, :, :DS].transpose(0, 2, 1, 3).reshape(ms, SECTION)

        g_rows = jnp.tile(diff_norm_g[l], (DS, 1))
        token_head_rows = lambda t: t.reshape(DB, DS * N_DIFF_HEADS, DIFF_DV)
        do_s = _diff_sample(token_head_rows(dqs), token_head_rows(dks), token_head_rows(dvs), kc, vc, page_table, l,
                            lam_params, g_rows, lam_init)
        xs = _merge(xs, zs, ro_s, do_s.reshape(ms, N_DIFF_HEADS * DIFF_DV), w_out_bf, l, g_mix_post[l], tm=ms)
        outs["s_dk"].append(dks.reshape(DB, DS, N_DIFF_HEADS, DIFF_DV))
        outs["s_dv"].append(dvs.reshape(DB, DS, N_DIFF_HEADS, DIFF_DV))

        mk, mv = _project(mem2d, g_mem_kv[l], w_mkv_bf, l, no_tabs_mem, tm=B * n_mem, tn=D, n_main=1)
        mk = mk.reshape(B, n_mem, D)
        mv = mv.reshape(B, n_mem, D)
        xp = _mem_prompt(xp, g_mem_pre[l], w_mq_bf, mk, mv, w_mo_bf, l, g_mem_post[l], S, tm=tm_p)
        qs, = _project(xs, g_mem_pre[l], w_mq_bf, l, no_tabs_s, tm=ms, tn=SECTION)
        q_rows = qs.reshape(DB, DS * N_MEM_HEADS, 2, LANES).transpose(0, 2, 1, 3)
        om = _mem_sample(q_rows.reshape(DB, 2 * DS * N_MEM_HEADS, LANES), mkc, mvc, l, bb=4)
        xs = _out_proj(xs, om.reshape(ms, D), w_mo_bf, l, g_mem_post[l], tm=ms)
        outs["p_mk"].append(mk.reshape(B, n_mem, N_MEM_HEADS, mem_hd))
        outs["p_mv"].append(mv.reshape(B, n_mem, N_MEM_HEADS, mem_hd))

        xp, tails = _conv_ffn_rows(xp, g_ffn_pre[l], w_up_bf, conv_w[l], conv_b[l], w_down_bf, l, zero_hist,
                                   g_ffn_post[l], tm=tm_p, tf=tf, blocks_per_seq=S // tm_p)
        outs["p_conv"].append(tails[jnp.arange(B) * (S // tm_p) + (S // tm_p - 1)])
        xs_t = xs.reshape(DB, DS, D).transpose(1, 0, 2).reshape(ms, D)
        hist_s = state_conv[l].transpose(1, 0, 2).reshape(1, 2 * DB, two_f)
        xs_t, tg, tv = _conv_ffn(xs_t, g_ffn_pre[l], w_up_bf, conv_w[l], conv_b[l], w_down_bf, l, hist_s,
                                 g_ffn_post[l], tm=ms, tf=tf, shift=DB, blocks_per_seq=1)
        xs = xs_t.reshape(DS, DB, D).transpose(1, 0, 2).reshape(ms, D)
        outs["s_conv"].append(jnp.concatenate([tg[0], tv[0]], axis=-1).reshape(2, DB, two_f).transpose(1, 0, 2))

    st = lambda k: jnp.stack(outs[k])
    p_dk, p_dv = (t.reshape(depth, B, S, N_DIFF_HEADS, DIFF_DV) for t in kv_stacks)
    return (xp.reshape(B, S, D), xs.reshape(DB, DS, D), st("p_ret"), p_dk, p_dv, st("p_mk"), st("p_mv"),
            st("p_conv"), s_ret, st("s_dk"), st("s_dv"), st("s_conv"))
```

```python
import functools
import math

import jax
import jax.numpy as jnp
from jax import lax
from jax.experimental import pallas as pl
from jax.experimental.pallas import tpu as pltpu

F32 = jnp.float32
BF16 = jnp.bfloat16

HEAD_DIM = 64
N_RET_HEADS = 8
N_DIFF_HEADS = 4
DIFF_DV = 2 * HEAD_DIM
SECTION = 512
N_SECTIONS = 7
N_MEM_HEADS = 4
CONV_WIDTH = 3
ROPE_THETA = 10000.0
EPS = 1e-6
NEG_BIG = -1e30
LOG2_E = math.log2(math.e)
RET_CHUNK = 128

LANES = 128
BF16_SUBLANES = 16
ONES_ROWS = BF16_SUBLANES
V7X_VMEM_BYTES = 64 * 1024 * 1024
VMEM_LIMIT = (V7X_VMEM_BYTES * 3) // 4


def _cparams(semantics):
    return pltpu.CompilerParams(dimension_semantics=semantics, vmem_limit_bytes=VMEM_LIMIT)


def _layer_spec(w, layer):
    zeros = (0,) * (w.ndim - 1)
    return pl.BlockSpec((None,) + w.shape[1:], lambda *_: (layer,) + zeros, pipeline_mode=pl.Buffered(1))


def _rms(x, g):
    return x * lax.rsqrt(jnp.mean(x * x, axis=-1, keepdims=True) + EPS) * g


def _dot(a, b):
    return jnp.dot(a, b, preferred_element_type=F32)


def _dot_nt(a, b):
    return lax.dot_general(a, b, (((1,), (1,)), ((), ())), preferred_element_type=F32)


def _silu(x):
    return x * (1.0 / (1.0 + jnp.exp(-x)))


def _proj_kernel(x_ref, g_ref, w_ref, cos_ref, sa_ref, sb_ref, *out_and_scratch, rope_sections, k_section,
                 n_main, tn):
    outs, xn_ref = out_and_scratch[:-1], out_and_scratch[-1]
    xn_ref[...] = _rms(x_ref[...], g_ref[...]).astype(BF16)
    n_sections = w_ref.shape[1] // tn

    def matmul(sec):
        return _dot(xn_ref[...], w_ref[:, sec * tn:(sec + 1) * tn])

    def emit(sec, acc):
        o_ref, col0 = (outs[0], sec * tn) if sec < n_main else (outs[sec - n_main + 1], 0)
        rope = sec in rope_sections
        scale = HEAD_DIM ** -0.5 if sec == k_section else 1.0
        if rope:
            c, sa, sb = cos_ref[...], sa_ref[...], sb_ref[...]
        for t in range(tn // LANES):
            r = acc[:, t * LANES:(t + 1) * LANES]
            if rope:
                r = r * c + pltpu.roll(r, LANES - HEAD_DIM // 2, 1) * sa + pltpu.roll(r, HEAD_DIM // 2, 1) * sb
                r = r * scale if scale != 1.0 else r
            o_ref[:, col0 + t * LANES:col0 + (t + 1) * LANES] = r

    acc_next = matmul(0)
    for sec in range(n_sections):
        acc = acc_next
        if sec + 1 < n_sections:
            acc_next = matmul(sec + 1)
        emit(sec, acc)


def _project(x, g, w_bf, layer, tabs, *, tm, tn, rope_sections=(), k_section=-1, n_main=None):
    m, d = x.shape
    n = w_bf.shape[2]
    n_sec = n // tn
    n_main = n_sec if n_main is None else n_main
    cos, sa, sb = tabs
    tab_blocks = cos.shape[0] // tm
    tab_spec = pl.BlockSpec((tm, LANES), lambda i: (i % tab_blocks, 0))
    out_shape = [jax.ShapeDtypeStruct((m, n_main * tn), F32)]
    out_specs = [pl.BlockSpec((tm, n_main * tn), lambda i: (i, 0))]
    for _ in range(n_main, n_sec):
        out_shape.append(jax.ShapeDtypeStruct((m, tn), F32))
        out_specs.append(pl.BlockSpec((tm, tn), lambda i: (i, 0)))
    kern = functools.partial(_proj_kernel, rope_sections=tuple(rope_sections), k_section=k_section, n_main=n_main,
                             tn=tn)
    return pl.pallas_call(
        kern,
        grid=(m // tm,),
        in_specs=[pl.BlockSpec((tm, d), lambda i: (i, 0)),
                  pl.BlockSpec((1, d), lambda i: (0, 0)),
                  _layer_spec(w_bf, layer),
                  tab_spec, tab_spec, tab_spec],
        out_specs=out_specs,
        out_shape=out_shape,
        scratch_shapes=[pltpu.VMEM((tm, d), BF16)],
        compiler_params=_cparams(("parallel",)),
        name="project",
    )(x, g.reshape(1, d), w_bf, cos, sa, sb)


def _ret_prompt_kernel(q_ref, k_ref, v_ref, dec_ref, rw_ref, ww_ref, cd_ref, g_ref, o_ref, sfin_ref, s_ref, *,
                       n_chunks):
    r = pl.program_id(1)

    @pl.when(r == 0)
    def _():
        s_ref[...] = jnp.zeros_like(s_ref)

    lane = lax.broadcasted_iota(jnp.int32, (RET_CHUNK, LANES), 1)
    first = lane < HEAD_DIM
    srow = lax.broadcasted_iota(jnp.int32, (LANES, LANES), 0)
    scol = lax.broadcasted_iota(jnp.int32, (LANES, LANES), 1)
    same_head = (srow < HEAD_DIM) == (scol < HEAD_DIM)

    def half_mean(t):
        tot = jnp.sum(t, axis=-1, keepdims=True)
        lo = jnp.sum(jnp.where(first, t, 0.0), axis=-1, keepdims=True)
        return jnp.where(first, lo, tot - lo) * (1.0 / HEAD_DIM)

    def chunk(c, carry):
        row = pl.multiple_of(c * RET_CHUNK, RET_CHUNK)
        rows = pl.ds(row, RET_CHUNK)
        for p in range(N_RET_HEADS // 2):
            cols = slice(p * LANES, (p + 1) * LANES)
            q = q_ref[rows, cols]
            k = k_ref[rows, cols]
            vb = v_ref[rows, cols].astype(BF16)
            kb = k.astype(BF16)
            s = s_ref[p]
            sc_a = (_dot_nt(jnp.where(first, q, 0.0).astype(BF16), kb) * dec_ref[2 * p]).astype(BF16)
            sc_b = (_dot_nt(jnp.where(first, 0.0, q).astype(BF16), kb) * dec_ref[2 * p + 1]).astype(BF16)
            o = jnp.where(first, _dot(sc_a, vb), _dot(sc_b, vb)) + _dot(q.astype(BF16), s.astype(BF16)) * rw_ref[p]
            kw_t = (k * ww_ref[p]).T.astype(BF16)
            s_ref[p] = s * cd_ref[p] + jnp.where(same_head, _dot(kw_t, vb), 0.0)
            d = o - half_mean(o)
            o_ref[rows, cols] = d * lax.rsqrt(half_mean(d * d) + EPS) * g_ref[p]
        return carry

    lax.fori_loop(0, n_chunks, chunk, 0)

    @pl.when(r == pl.num_programs(1) - 1)
    def _():
        for h in range(N_RET_HEADS):
            lo = (h % 2) * HEAD_DIM
            sfin_ref[0, h] = s_ref[h // 2][lo:lo + HEAD_DIM, lo:lo + HEAD_DIM]


def _retention_tables(length):
    log_g = jnp.log1p(-jnp.exp2(-5.0 - jnp.arange(N_RET_HEADS, dtype=F32)))
    n = jnp.arange(length, dtype=F32)
    diff = n[:, None] - n[None, :]
    dec = jnp.where((diff >= 0)[None], jnp.exp(log_g[:, None, None] * jnp.maximum(diff, 0.0)[None]), 0.0)
    read_w = jnp.exp(log_g[:, None] * (n + 1.0)[None])
    write_w = jnp.exp(log_g[:, None] * (length - 1.0 - n)[None])
    chunk_decay = jnp.exp(log_g * length)
    return dec, read_w, write_w, chunk_decay


def _retention_prompt(z, ret_g, batch, seq, *, rows_per_step):
    m = z.shape[0]
    nr = seq // rows_per_step
    dec, read_w, write_w, chunk_decay = _retention_tables(RET_CHUNK)
    n_pairs = N_RET_HEADS // 2

    def per_lane(t):
        t = jnp.broadcast_to(t[:, :, None], (N_RET_HEADS, t.shape[1], HEAD_DIM))
        return t.reshape(n_pairs, 2, t.shape[1], HEAD_DIM).transpose(0, 2, 1, 3).reshape(n_pairs, t.shape[1], LANES)

    rw = per_lane(read_w)
    ww = per_lane(write_w)
    cd = jnp.repeat(per_lane(jnp.broadcast_to(chunk_decay[:, None], (N_RET_HEADS, HEAD_DIM))), 2, axis=1)
    full3 = lambda shape: pl.BlockSpec(shape, lambda b, r: (0, 0, 0))
    sec = lambda c: pl.BlockSpec((rows_per_step, SECTION), lambda b, r: (b * nr + r, c))
    return pl.pallas_call(
        functools.partial(_ret_prompt_kernel, n_chunks=rows_per_step // RET_CHUNK),
        grid=(batch, nr),
        in_specs=[sec(0), sec(1), sec(2),
                  full3((N_RET_HEADS, RET_CHUNK, RET_CHUNK)),
                  full3((n_pairs, RET_CHUNK, LANES)),
                  full3((n_pairs, RET_CHUNK, LANES)),
                  full3((n_pairs, LANES, LANES)),
                  full3((n_pairs, 1, LANES))],
        out_specs=[pl.BlockSpec((rows_per_step, SECTION), lambda b, r: (b * nr + r, 0)),
                   pl.BlockSpec((1, N_RET_HEADS, HEAD_DIM, HEAD_DIM), lambda b, r: (b, 0, 0, 0))],
        out_shape=[jax.ShapeDtypeStruct((m, SECTION), F32),
                   jax.ShapeDtypeStruct((batch, N_RET_HEADS, HEAD_DIM, HEAD_DIM), F32)],
        scratch_shapes=[pltpu.VMEM((n_pairs, LANES, LANES), F32)],
        compiler_params=_cparams(("parallel", "arbitrary")),
        name="retention_prompt",
    )(z, z, z, dec, rw, ww, cd, ret_g.reshape(n_pairs, 1, LANES))


def _ret_sample_kernel(q_ref, k_ref, v_ref, s_ref, dec_ref, rw_ref, ww_ref, cd_ref, g_ref, *rest):
    o_ref, snew_ref = rest[-2:]
    n = q_ref.shape[0] * q_ref.shape[1]
    length = q_ref.shape[2]
    q = q_ref[...].reshape(n, length, HEAD_DIM)
    k = k_ref[...].reshape(n, length, HEAD_DIM)
    vb = v_ref[...].reshape(n, length, HEAD_DIM).astype(BF16)
    s = s_ref[...].reshape(n, HEAD_DIM, HEAD_DIM)
    qb = q.astype(BF16)
    scores = jnp.einsum('nqd,nkd->nqk', qb, k.astype(BF16), preferred_element_type=F32) * dec_ref[...]
    o = jnp.einsum('nqk,nke->nqe', scores.astype(BF16), vb, preferred_element_type=F32)
    o = o + jnp.einsum('nqd,nde->nqe', qb, s.astype(BF16), preferred_element_type=F32) * rw_ref[...]
    kw = (k * ww_ref[...]).astype(BF16)
    s_new = s * cd_ref[...] + jnp.einsum('nld,nle->nde', kw, vb, preferred_element_type=F32)
    mu = jnp.mean(o, axis=-1, keepdims=True)
    var = jnp.mean(jnp.square(o - mu), axis=-1, keepdims=True)
    o_ref[...] = ((o - mu) * lax.rsqrt(var + EPS) * g_ref[...]).reshape(o_ref.shape)
    snew_ref[...] = s_new.reshape(snew_ref.shape)


def _retention_sample(q, k, v, state, layer, depth, new_states, ret_g, n_tokens, *, bb):
    db, nh, length, _ = q.shape
    prior = () if new_states is None else (new_states,)
    dec, read_w, write_w, chunk_decay = _retention_tables(n_tokens)
    pad = length - n_tokens
    dec = jnp.pad(dec, ((0, 0), (0, pad), (0, pad)))
    read_w = jnp.pad(read_w, ((0, 0), (0, pad)))
    write_w = jnp.pad(write_w, ((0, 0), (0, pad)))
    n = bb * nh
    tile = lambda t: jnp.tile(t, (bb,) + (1,) * (t.ndim - 1))
    dec_t = tile(dec)
    rw_t = tile(jnp.broadcast_to(read_w[:, :, None], (nh, length, HEAD_DIM)))
    ww_t = tile(jnp.broadcast_to(write_w[:, :, None], (nh, length, HEAD_DIM)))
    cd_t = tile(jnp.broadcast_to(chunk_decay[:, None, None], (nh, HEAD_DIM, HEAD_DIM)))
    g_t = tile(jnp.broadcast_to(ret_g[:, None, :], (nh, length, HEAD_DIM)))
    blk = pl.BlockSpec((bb, nh, length, HEAD_DIM), lambda i: (i, 0, 0, 0))
    sblk = pl.BlockSpec((bb, nh, HEAD_DIM, HEAD_DIM), lambda i: (i, 0, 0, 0))
    sout = pl.BlockSpec((None, bb, nh, HEAD_DIM, HEAD_DIM), lambda i: (layer, i, 0, 0, 0))
    full = lambda a: pl.BlockSpec(a.shape, lambda i: (0,) * a.ndim)
    return pl.pallas_call(
        _ret_sample_kernel,
        grid=(db // bb,),
        in_specs=[blk, blk, blk, sblk, full(dec_t), full(rw_t), full(ww_t), full(cd_t), full(g_t)]
        + [pl.BlockSpec(memory_space=pl.ANY)] * len(prior),
        out_specs=[blk, sout],
        out_shape=[jax.ShapeDtypeStruct(q.shape, F32), jax.ShapeDtypeStruct((depth,) + state.shape, F32)],
        input_output_aliases={9: 1} if prior else {},
        compiler_params=_cparams(("parallel",)),
        name="retention_sample",
    )(q, k, v, state, dec_t, rw_t, ww_t, cd_t, g_t, *prior)


def _lambda(lq1_ref, lk1_ref, lq2_ref, lk2_ref, lam_init):
    e1 = jnp.exp(jnp.sum(lq1_ref[...] * lk1_ref[...], axis=-1, keepdims=True))
    e2 = jnp.exp(jnp.sum(lq2_ref[...] * lk2_ref[...], axis=-1, keepdims=True))
    return e1 - e2 + lam_init


def _diff_prompt_kernel(qi_ref, ki_ref, q_ref, k_ref, v_ref, lq1_ref, lk1_ref, lq2_ref, lk2_ref, g_ref, *rest,
                        lam_init, heads):
    o_ref, ko_ref, vo_ref, m_ref, acc_ref, qm_ref = rest[-6:]
    p = pl.program_id(2)
    qi = qi_ref[p]
    ki = ki_ref[p]
    tk = k_ref.shape[0]

    @pl.when(ki == 0)
    def _():
        m_ref[...] = jnp.full_like(m_ref, NEG_BIG)
        acc_ref[...] = jnp.zeros_like(acc_ref)
        lane = lax.broadcasted_iota(jnp.int32, (q_ref.shape[0], DIFF_DV), 1)
        for h in range(heads):
            q = q_ref[:, h * DIFF_DV:(h + 1) * DIFF_DV] * (HEAD_DIM ** -0.5 * LOG2_E)
            qm_ref[2 * h] = jnp.where(lane < HEAD_DIM, q, 0.0).astype(BF16)
            qm_ref[2 * h + 1] = jnp.where(lane < HEAD_DIM, 0.0, q).astype(BF16)

    def step(masked):
        ones = jnp.ones((ONES_ROWS, tk), BF16)
        kb = [None] * heads

        def scores(idx):
            h = idx // 2
            if kb[h] is None:
                kb[h] = k_ref[:, h * DIFF_DV:(h + 1) * DIFF_DV].astype(BF16)
            return _dot_nt(kb[h], qm_ref[idx])

        def accumulate(idx, s, v_ext):
            if masked:
                key = lax.broadcasted_iota(jnp.int32, s.shape, 0)
                qry = lax.broadcasted_iota(jnp.int32, s.shape, 1)
                s = jnp.where(key <= qry, s, NEG_BIG)
            m_old = m_ref[idx]
            m_new = jnp.maximum(m_old, jnp.max(s, axis=0, keepdims=True))
            alpha = jnp.exp2(m_old - m_new)
            e = jnp.exp2(s - m_new).astype(BF16)
            acc_ref[idx] = alpha * acc_ref[idx] + _dot(v_ext, e)
            m_ref[idx] = m_new

        n = 2 * heads
        s_next = scores(0)
        v_ext = None
        for idx in range(n):
            s_cur = s_next
            if idx + 1 < n:
                s_next = scores(idx + 1)
            if idx % 2 == 0:
                hcols = slice((idx // 2) * DIFF_DV, (idx // 2 + 1) * DIFF_DV)
                v_ext = jnp.concatenate([v_ref[:, hcols].T.astype(BF16), ones], axis=0)
            accumulate(idx, s_cur, v_ext)

    @pl.when(ki < qi)
    def _():
        step(False)

    @pl.when(ki == qi)
    def _():
        step(True)
        for h in range(heads):
            ko_ref[:, h, :] = k_ref[:, h * DIFF_DV:(h + 1) * DIFF_DV]
            vo_ref[:, h, :] = v_ref[:, h * DIFF_DV:(h + 1) * DIFF_DV]
        lam = _lambda(lq1_ref, lk1_ref, lq2_ref, lk2_ref, lam_init)
        for h in range(heads):
            a1 = acc_ref[2 * h]
            a2 = acc_ref[2 * h + 1]
            o_t = (a1[:DIFF_DV] / a1[DIFF_DV:DIFF_DV + 1] - lam * (a2[:DIFF_DV] / a2[DIFF_DV:DIFF_DV + 1]))
            o_ref[:, h * DIFF_DV:(h + 1) * DIFF_DV] = _rms(o_t.T, g_ref[h]) * (1.0 - lam_init)


def _diff_prompt(z, dk, dv, lam_params, diff_g, lam_init, batch, seq, layer, depth, stacks, *, tile, heads):
    m = z.shape[0]
    nt = seq // tile
    width = heads * DIFF_DV
    groups = N_DIFF_HEADS // heads
    assert groups == 1
    pairs = [(qi, ki) for qi in range(nt) for ki in range(qi + 1)]
    qi_tab = jnp.asarray([p[0] for p in pairs], jnp.int32)
    ki_tab = jnp.asarray([p[1] for p in pairs], jnp.int32)
    q_col0 = 4 * SECTION // width
    stacks = () if stacks is None else tuple(stacks)
    lam_spec = pl.BlockSpec((1, HEAD_DIM), lambda b, h, p, qi, ki: (0, 0))
    stack_spec = pl.BlockSpec((None, tile, N_DIFF_HEADS, DIFF_DV),
                              lambda b, h, p, qi, ki: (layer, b * nt + qi[p], 0, 0))
    grid_spec = pltpu.PrefetchScalarGridSpec(
        num_scalar_prefetch=2,
        grid=(batch, groups, len(pairs)),
        in_specs=[pl.BlockSpec((tile, width), lambda b, h, p, qi, ki: (b * nt + qi[p], q_col0 + h)),
                  pl.BlockSpec((tile, width), lambda b, h, p, qi, ki: (b * nt + ki[p], h)),
                  pl.BlockSpec((tile, width), lambda b, h, p, qi, ki: (b * nt + ki[p], h)),
                  lam_spec, lam_spec, lam_spec, lam_spec,
                  pl.BlockSpec((heads, 1, DIFF_DV), lambda b, h, p, qi, ki: (h, 0, 0))]
        + [pl.BlockSpec(memory_space=pl.ANY)] * len(stacks),
        out_specs=[pl.BlockSpec((tile, width), lambda b, h, p, qi, ki: (b * nt + qi[p], h)), stack_spec, stack_spec],
        scratch_shapes=[pltpu.VMEM((2 * heads, 1, tile), F32),
                        pltpu.VMEM((2 * heads, DIFF_DV + ONES_ROWS, tile), F32),
                        pltpu.VMEM((2 * heads, tile, DIFF_DV), BF16)],
    )
    stack_shape = jax.ShapeDtypeStruct((depth, m, N_DIFF_HEADS, DIFF_DV), F32)
    n_in = 2 + 3 + len(lam_params) + 1
    return pl.pallas_call(
        functools.partial(_diff_prompt_kernel, lam_init=lam_init, heads=heads),
        grid_spec=grid_spec,
        out_shape=[jax.ShapeDtypeStruct((m, N_DIFF_HEADS * DIFF_DV), F32), stack_shape, stack_shape],
        input_output_aliases={n_in + s: 1 + s for s in range(len(stacks))},
        compiler_params=_cparams(("parallel", "parallel", "arbitrary")),
        name="diff_attn_prompt",
    )(qi_tab, ki_tab, z, dk, dv, *lam_params, diff_g.reshape(N_DIFF_HEADS, 1, DIFF_DV), *stacks)


def _diff_sample_kernel(pt_ref, q_ref, kn_ref, vn_ref, *rest, n_pages, lam_init):
    k_pages = rest[:n_pages]
    v_pages = rest[n_pages:2 * n_pages]
    g_ref, lq1_ref, lk1_ref, lq2_ref, lk2_ref, o_ref, s_ref = rest[2 * n_pages:]
    half = q_ref.shape[1]
    n_rows = 2 * half
    page_rows = k_pages[0].shape[0]
    qf = q_ref[0] * (HEAD_DIM ** -0.5)
    lane = lax.broadcasted_iota(jnp.int32, qf.shape, 1)
    q = jnp.concatenate([jnp.where(lane < HEAD_DIM, qf, 0.0), jnp.where(lane < HEAD_DIM, 0.0, qf)],
                        axis=0).astype(BF16)

    row_h = lax.broadcasted_iota(jnp.int32, (n_rows, page_rows), 0) % N_DIFF_HEADS
    col_h = lax.broadcasted_iota(jnp.int32, (n_rows, page_rows), 1) % N_DIFF_HEADS
    same_head = row_h == col_h
    m = jnp.full((n_rows, 1), NEG_BIG, F32)
    for p in range(n_pages):
        s = jnp.where(same_head, _dot_nt(q, k_pages[p][...].astype(BF16)), NEG_BIG)
        s_ref[:, p * page_rows:(p + 1) * page_rows] = s
        m = jnp.maximum(m, jnp.max(s, axis=-1, keepdims=True))

    n_new = kn_ref.shape[1]
    r = lax.broadcasted_iota(jnp.int32, (n_rows, n_new), 0)
    c = lax.broadcasted_iota(jnp.int32, (n_rows, n_new), 1)
    ok = (r % N_DIFF_HEADS == c % N_DIFF_HEADS) & (c // N_DIFF_HEADS <= (r % half) // N_DIFF_HEADS)
    s_new = jnp.where(ok, _dot_nt(q, kn_ref[0].astype(BF16)), NEG_BIG)
    m = jnp.maximum(m, jnp.max(s_new, axis=-1, keepdims=True))

    e_new = jnp.exp(s_new - m)
    l = jnp.sum(e_new, axis=-1, keepdims=True)
    for p in range(n_pages):
        cols = slice(p * page_rows, (p + 1) * page_rows)
        e = jnp.exp(s_ref[:, cols] - m)
        s_ref[:, cols] = e
        l = l + jnp.sum(e, axis=-1, keepdims=True)

    lam = _lambda(lq1_ref, lk1_ref, lq2_ref, lk2_ref, lam_init)
    inv = 1.0 / l
    w1 = inv[:half]
    w2 = lam * inv[half:]
    a_new = (e_new[:half] * w1 - e_new[half:] * w2).astype(BF16)
    o = _dot(a_new, vn_ref[0].astype(BF16))
    for p in range(n_pages):
        cols = slice(p * page_rows, (p + 1) * page_rows)
        a = (s_ref[:half, cols] * w1 - s_ref[half:, cols] * w2).astype(BF16)
        o = o + _dot(a, v_pages[p][...].astype(BF16))
    o_ref[0] = _rms(o, g_ref[...]) * (1.0 - lam_init)


def _diff_sample(q_new, k_new, v_new, k_cache, v_cache, page_table, layer, lam_params, g_rows, lam_init):
    db = q_new.shape[0]
    n_rows = 2 * q_new.shape[1]
    n_pages = page_table.shape[1]
    page_rows = k_cache.shape[2]
    n_new = k_new.shape[1]

    def page_spec(p):
        return pl.BlockSpec((None, None, page_rows, LANES), lambda b, pt: (layer, pt[b * n_pages + p], 0, 0))

    small = lambda rows: pl.BlockSpec((1, rows, LANES), lambda b, pt: (b, 0, 0))
    lam_spec = pl.BlockSpec((1, HEAD_DIM), lambda b, pt: (0, 0))
    grid_spec = pltpu.PrefetchScalarGridSpec(
        num_scalar_prefetch=1,
        grid=(db,),
        in_specs=[small(n_new), small(n_new), small(n_new)]
        + [page_spec(p) for p in range(n_pages)] * 2
        + [pl.BlockSpec((n_rows // 2, LANES), lambda b, pt: (0, 0)), lam_spec, lam_spec, lam_spec, lam_spec],
        out_specs=small(n_rows // 2),
        scratch_shapes=[pltpu.VMEM((n_rows, n_pages * page_rows), F32)],
    )
    return pl.pallas_call(
        functools.partial(_diff_sample_kernel, n_pages=n_pages, lam_init=lam_init),
        grid_spec=grid_spec,
        out_shape=jax.ShapeDtypeStruct((db, n_rows // 2, LANES), F32),
        compiler_params=_cparams(("parallel",)),
        name="diff_attn_sample",
    )(page_table.reshape(-1), q_new, k_new, v_new, *([k_cache] * n_pages), *([v_cache] * n_pages), g_rows,
      *lam_params)


def _merge_kernel(x_ref, rg_ref, ro_ref, do_ref, w_ref, g_ref, o_ref):
    a = jnp.concatenate([_silu(rg_ref[...]) * ro_ref[...], do_ref[...]], axis=-1).astype(BF16)
    o_ref[...] = x_ref[...] + _rms(_dot(a, w_ref[...]), g_ref[...])


def _merge(x, z, ro, do, w_bf, layer, g, *, tm):
    m, d = x.shape
    rg_col = 3
    row = lambda w: pl.BlockSpec((tm, w), lambda i: (i, 0))
    return pl.pallas_call(
        _merge_kernel,
        grid=(m // tm,),
        in_specs=[row(d), pl.BlockSpec((tm, SECTION), lambda i: (i, rg_col)), row(SECTION), row(SECTION),
                  _layer_spec(w_bf, layer), pl.BlockSpec((1, d), lambda i: (0, 0))],
        out_specs=row(d),
        out_shape=jax.ShapeDtypeStruct((m, d), F32),
        compiler_params=_cparams(("parallel",)),
        name="mixer_merge",
    )(x, z, ro, do, w_bf, g.reshape(1, d))


def _mem_prompt_kernel(x_ref, gpre_ref, wq_ref, mk_ref, mv_ref, wo_ref, gpost_ref, o_ref):
    x = x_ref[...]
    q = _dot(_rms(x, gpre_ref[...]).astype(BF16), wq_ref[...])
    hd = q.shape[1] // N_MEM_HEADS

    def scores(h):
        cols = slice(h * hd, (h + 1) * hd)
        return _dot_nt(q[:, cols].astype(BF16), mk_ref[0, :, cols].astype(BF16)) * (hd ** -0.5)

    outs = []
    s_next = scores(0)
    for h in range(N_MEM_HEADS):
        s = s_next
        if h + 1 < N_MEM_HEADS:
            s_next = scores(h + 1)
        e = jnp.exp(s - jnp.max(s, axis=-1, keepdims=True))
        p = e / jnp.sum(e, axis=-1, keepdims=True)
        outs.append(_dot(p.astype(BF16), mv_ref[0, :, h * hd:(h + 1) * hd].astype(BF16)))
    o = jnp.concatenate(outs, axis=-1).astype(BF16)
    o_ref[...] = x + _rms(_dot(o, wo_ref[...]), gpost_ref[...])


def _mem_prompt(x, g_pre, wq_bf, mk, mv, wo_bf, layer, g_post, seq, *, tm):
    m, d = x.shape
    n_mem = mk.shape[1]
    blocks_per_seq = seq // tm
    vec = pl.BlockSpec((1, d), lambda i: (0, 0))
    mem = pl.BlockSpec((1, n_mem, d), lambda i: (i // blocks_per_seq, 0, 0))
    row = pl.BlockSpec((tm, d), lambda i: (i, 0))
    return pl.pallas_call(
        _mem_prompt_kernel,
        grid=(m // tm,),
        in_specs=[row, vec, _layer_spec(wq_bf, layer), mem, mem, _layer_spec(wo_bf, layer), vec],
        out_specs=row,
        out_shape=jax.ShapeDtypeStruct((m, d), F32),
        compiler_params=_cparams(("parallel",)),
        name="mem_attn_prompt",
    )(x, g_pre.reshape(1, d), wq_bf, mk, mv, wo_bf, g_post.reshape(1, d))


def _mem_sample_kernel(q_ref, k_ref, v_ref, o_ref):
    bb, n_rows, _ = q_ref.shape
    half = n_rows // 2
    key_rows = k_ref.shape[1]
    group = 2 * N_MEM_HEADS
    row = lax.broadcasted_iota(jnp.int32, (half, key_rows), 0)
    col = lax.broadcasted_iota(jnp.int32, (half, key_rows), 1)
    own = (col % group) == (row % N_MEM_HEADS)
    scale = (2 * LANES) ** -0.5
    for b in range(bb):
        sp = _dot_nt(q_ref[b].astype(BF16), k_ref[b].astype(BF16))
        s = (sp[:half] + pltpu.roll(sp[half:], key_rows - N_MEM_HEADS, 1)) * scale
        s = jnp.where(own, s, NEG_BIG)
        e = jnp.exp(s - jnp.max(s, axis=-1, keepdims=True))
        p = e / jnp.sum(e, axis=-1, keepdims=True)
        p2 = jnp.concatenate([p, pltpu.roll(p, N_MEM_HEADS, 1)], axis=0).astype(BF16)
        o = _dot(p2, v_ref[b].astype(BF16))
        o_ref[b] = jnp.concatenate([o[:half], o[half:]], axis=-1)


def _mem_sample(q_rows, k_cache, v_cache, layer, *, bb):
    db, n_rows, _ = q_rows.shape
    key_rows = k_cache.shape[2]
    qspec = pl.BlockSpec((bb, n_rows, LANES), lambda i: (i, 0, 0))
    cspec = pl.BlockSpec((None, bb, key_rows, LANES), lambda i: (layer, i, 0, 0))
    return pl.pallas_call(
        _mem_sample_kernel,
        grid=(db // bb,),
        in_specs=[qspec, cspec, cspec],
        out_specs=pl.BlockSpec((bb, n_rows // 2, 2 * LANES), lambda i: (i, 0, 0)),
        out_shape=jax.ShapeDtypeStruct((db, n_rows // 2, 2 * LANES), F32),
        compiler_params=_cparams(("parallel",)),
        name="mem_attn_sample",
    )(q_rows, k_cache, v_cache)


def _out_proj_kernel(x_ref, a_ref, w_ref, g_ref, o_ref):
    o_ref[...] = x_ref[...] + _rms(_dot(a_ref[...].astype(BF16), w_ref[...]), g_ref[...])


def _out_proj(x, a, w_bf, layer, g, *, tm):
    m, d = x.shape
    row = pl.BlockSpec((tm, d), lambda i: (i, 0))
    return pl.pallas_call(
        _out_proj_kernel,
        grid=(m // tm,),
        in_specs=[row, pl.BlockSpec((tm, a.shape[1]), lambda i: (i, 0)), _layer_spec(w_bf, layer),
                  pl.BlockSpec((1, d), lambda i: (0, 0))],
        out_specs=row,
        out_shape=jax.ShapeDtypeStruct((m, d), F32),
        compiler_params=_cparams(("parallel",)),
        name="out_proj",
    )(x, a, w_bf, g.reshape(1, d))


def _ffn_kernel(x_ref, gpre_ref, wug_ref, wuv_ref, cwg_ref, cwv_ref, cbg_ref, cbv_ref, wd_ref, hg_ref, hv_ref,
                gpost_ref, o_ref, tg_ref, tv_ref, xn_ref, acc_ref, halo_g_ref, halo_v_ref, *, shift,
                blocks_per_seq):
    i = pl.program_id(0)
    j = pl.program_id(1)
    tm = x_ref.shape[0]

    @pl.when(j == 0)
    def _():
        xn_ref[...] = _rms(x_ref[...], gpre_ref[...]).astype(BF16)
        acc_ref[...] = jnp.zeros_like(acc_ref)

    @pl.when((i % blocks_per_seq) == 0)
    def _():
        halo_g_ref[j] = hg_ref[0]
        halo_v_ref[j] = hv_ref[0]

    def conv(w_ref, cw_ref, cb_ref, halo_ref, tail_ref):
        u = _dot(xn_ref[...], w_ref[...])
        prev = halo_ref[j]
        ue = jnp.concatenate([prev, u], axis=0)
        u2 = ue[:tm]
        u1 = ue[shift:shift + tm]
        tail = u[tm - 2 * shift:]
        halo_ref[j] = tail
        tail_ref[0] = tail
        cw = cw_ref[...]
        return cb_ref[...] + cw[0:1] * u2 + cw[1:2] * u1 + cw[2:3] * u

    gate = conv(wug_ref, cwg_ref, cbg_ref, halo_g_ref, tg_ref)
    val = conv(wuv_ref, cwv_ref, cbv_ref, halo_v_ref, tv_ref)
    acc_ref[...] += _dot((_silu(gate) * val).astype(BF16), wd_ref[...])

    @pl.when(j == pl.num_programs(1) - 1)
    def _():
        o_ref[...] = x_ref[...] + _rms(acc_ref[...], gpost_ref[...])


def _ffn_rows_kernel(x_ref, gpre_ref, wu_ref, cw_ref, cb_ref, wd_ref, hist_ref, gpost_ref, o_ref, tail_ref,
                     xn_ref, h_ref, halo_ref, *, tf, blocks_per_seq):
    i = pl.program_id(0)
    tm = x_ref.shape[0]
    f = wd_ref.shape[0]
    n_tiles = f // tf
    xn_ref[...] = _rms(x_ref[...], gpre_ref[...]).astype(BF16)
    rows = lax.broadcasted_iota(jnp.int32, (tm, tf), 0)

    @pl.when((i % blocks_per_seq) == 0)
    def _():
        halo_ref[...] = hist_ref[0]

    def up(col):
        return _dot(xn_ref[...], wu_ref[:, col:col + tf])

    def conv(u, col):
        cols = slice(col, col + tf)
        prev = halo_ref[:, cols]
        u1 = jnp.where(rows == 0, prev[1:2], pltpu.roll(u, 1, 0))
        u2 = jnp.where(rows == 0, prev[0:1], jnp.where(rows == 1, prev[1:2], pltpu.roll(u, 2, 0)))
        tail = u[tm - 2:]
        halo_ref[:, cols] = tail
        tail_ref[0, :, cols] = tail
        cw = cw_ref[:, cols]
        return cb_ref[:, cols] + cw[0:1] * u2 + cw[1:2] * u1 + cw[2:3] * u

    ug, uv = up(0), up(f)
    for t in range(n_tiles):
        cg, cv = ug, uv
        if t + 1 < n_tiles:
            ug, uv = up((t + 1) * tf), up(f + (t + 1) * tf)
        gate = conv(cg, t * tf)
        val = conv(cv, f + t * tf)
        h_ref[:, t * tf:(t + 1) * tf] = (_silu(gate) * val).astype(BF16)
    o_ref[...] = x_ref[...] + _rms(_dot(h_ref[...], wd_ref[...]), gpost_ref[...])


def _conv_ffn_rows(x, g_pre, wu_bf, conv_w, conv_b, wd_bf, layer, hist, g_post, *, tm, tf, blocks_per_seq):
    m, d = x.shape
    f = wd_bf.shape[1]
    nblk = m // tm
    halo = CONV_WIDTH - 1
    const = lambda a: pl.BlockSpec(a.shape, lambda i: (0,) * a.ndim, pipeline_mode=pl.Buffered(1))
    row = pl.BlockSpec((tm, d), lambda i: (i, 0))
    g_pre, g_post, conv_b = g_pre.reshape(1, d), g_post.reshape(1, d), conv_b.reshape(1, 2 * f)
    return pl.pallas_call(
        functools.partial(_ffn_rows_kernel, tf=tf, blocks_per_seq=blocks_per_seq),
        grid=(nblk,),
        in_specs=[row, const(g_pre), _layer_spec(wu_bf, layer), const(conv_w), const(conv_b),
                  _layer_spec(wd_bf, layer),
                  pl.BlockSpec((1, halo, 2 * f), lambda i: (i // blocks_per_seq, 0, 0)), const(g_post)],
        out_specs=[row, pl.BlockSpec((1, halo, 2 * f), lambda i: (i, 0, 0))],
        out_shape=[jax.ShapeDtypeStruct((m, d), F32), jax.ShapeDtypeStruct((nblk, halo, 2 * f), F32)],
        scratch_shapes=[pltpu.VMEM((tm, d), BF16), pltpu.VMEM((tm, f), BF16), pltpu.VMEM((halo, 2 * f), F32)],
        compiler_params=_cparams(("arbitrary",)),
        name="conv_ffn_rows",
    )(x, g_pre, wu_bf, conv_w, conv_b, wd_bf, hist, g_post)


def _conv_ffn(x, g_pre, wu_bf, conv_w, conv_b, wd_bf, layer, hist, g_post, *, tm, tf, shift, blocks_per_seq):
    m, d = x.shape
    assert shift % 8 == 0 and tm >= 2 * shift
    f = wd_bf.shape[1]
    nj = f // tf
    nblk = m // tm
    halo = 2 * shift
    up = lambda off: pl.BlockSpec((None, d, tf), lambda i, j: (layer, 0, off + j))
    cw = lambda off: pl.BlockSpec((CONV_WIDTH, tf), lambda i, j: (0, off + j))
    cb = lambda off: pl.BlockSpec((1, tf), lambda i, j: (0, off + j))
    hs = lambda off: pl.BlockSpec((1, halo, tf), lambda i, j: (i // blocks_per_seq, 0, off + j))
    vec = pl.BlockSpec((1, d), lambda i, j: (0, 0))
    row = pl.BlockSpec((tm, d), lambda i, j: (i, 0))
    tail = pl.BlockSpec((1, halo, tf), lambda i, j: (i, 0, j))
    return pl.pallas_call(
        functools.partial(_ffn_kernel, shift=shift, blocks_per_seq=blocks_per_seq),
        grid=(nblk, nj),
        in_specs=[row, vec, up(0), up(nj), cw(0), cw(nj), cb(0), cb(nj),
                  pl.BlockSpec((None, tf, d), lambda i, j: (layer, j, 0)), hs(0), hs(nj), vec],
        out_specs=[row, tail, tail],
        out_shape=[jax.ShapeDtypeStruct((m, d), F32),
                   jax.ShapeDtypeStruct((nblk, halo, f), F32),
                   jax.ShapeDtypeStruct((nblk, halo, f), F32)],
        scratch_shapes=[pltpu.VMEM((tm, d), BF16), pltpu.VMEM((tm, d), F32),
                        pltpu.VMEM((nj, halo, tf), F32), pltpu.VMEM((nj, halo, tf), F32)],
        compiler_params=_cparams(("arbitrary", "arbitrary")),
        name="conv_ffn",
    )(x, g_pre.reshape(1, d), wu_bf, wu_bf, conv_w, conv_w, conv_b.reshape(1, -1), conv_b.reshape(1, -1), wd_bf,
      hist, hist, g_post.reshape(1, d))


def _rope_tables(pos):
    inv = 1.0 / (ROPE_THETA ** (jnp.arange(0, HEAD_DIM, 2, dtype=F32) / HEAD_DIM))
    ang = pos.astype(F32)[:, None] * inv[None, :]
    ang = jnp.concatenate([ang, ang], axis=-1)
    cos, sin = jnp.cos(ang), jnp.sin(ang)
    first_half = jnp.arange(HEAD_DIM) < HEAD_DIM // 2
    sin_a = jnp.where(first_half[None], -sin, 0.0)
    sin_b = jnp.where(first_half[None], 0.0, sin)
    rep = LANES // HEAD_DIM
    return tuple(jnp.tile(t, (1, rep)) for t in (cos, sin_a, sin_b))


def _tile_for(n, target):
    t = min(n, target)
    assert n % t == 0, (n, t)
    return t


def kernel(x_prompt, x_sample, mem_prompt, state_ret, cache_diff_k, cache_diff_v, page_table, cache_mem_k, cache_mem_v, state_conv, g_mix_pre, g_mix_post, w_in, ret_norm_g, lam_q1, lam_k1, lam_q2, lam_k2, diff_norm_g, w_out, g_mem_pre, g_mem_post, g_mem_kv, w_mq, w_mk, w_mv, w_mo, g_ffn_pre, g_ffn_post, w_up, conv_w, conv_b, w_down):
    B, S, D = x_prompt.shape
    DB, DS, _ = x_sample.shape
    depth = w_in.shape[0]
    n_pool, page = cache_diff_k.shape[1], cache_diff_k.shape[2]
    past = page_table.shape[1] * page
    n_mem = mem_prompt.shape[1]
    two_f = w_up.shape[2]
    mem_hd = D // N_MEM_HEADS
    mp, ms = B * S, DB * DS

    tm_p = _tile_for(S, 512)
    tile_attn = _tile_for(S, 512)
    tf = 256
    assert (two_f // 2) % tf == 0
    assert mem_hd == 2 * LANES

    tabs_p = _rope_tables(jnp.arange(S, dtype=jnp.int32))
    tabs_s = _rope_tables(jnp.tile(past + jnp.arange(DS, dtype=jnp.int32), DB))
    no_tabs_mem = tuple(jnp.zeros((B * n_mem, LANES), F32) for _ in range(3))
    no_tabs_s = tuple(jnp.zeros((ms, LANES), F32) for _ in range(3))

    kc = cache_diff_k.reshape(depth, n_pool, page * N_DIFF_HEADS, DIFF_DV)
    vc = cache_diff_v.reshape(depth, n_pool, page * N_DIFF_HEADS, DIFF_DV)

    def mem_rows(c):
        c = c.reshape(depth, DB, n_mem, N_MEM_HEADS, 2, LANES).transpose(0, 1, 2, 4, 3, 5)
        return c.reshape(depth, DB, n_mem * 2 * N_MEM_HEADS, LANES)

    mkc = mem_rows(cache_mem_k)
    mvc = mem_rows(cache_mem_v)

    xp = x_prompt.reshape(mp, D)
    xs = x_sample.reshape(ms, D)
    mem2d = mem_prompt.reshape(B * n_mem, D)
    zero_hist = jnp.zeros((B, 2, two_f), F32)
    ret_pad = 8

    outs = {k: [] for k in ("p_ret", "p_mk", "p_mv", "p_conv", "s_dk", "s_dv", "s_conv")}
    kv_stacks = None
    s_ret = None
    rope_secs = (0, 1, 4, 5)
    w_in_bf = w_in.astype(BF16)
    w_out_bf = w_out.astype(BF16)
    w_mq_bf = w_mq.astype(BF16)
    w_mo_bf = w_mo.astype(BF16)
    w_mkv_bf = jnp.concatenate([w_mk, w_mv], axis=2).astype(BF16)
    w_up_bf = w_up.astype(BF16)
    w_down_bf = w_down.astype(BF16)
    for l in range(depth):
        lam_init = 0.8 - 0.6 * math.exp(-0.3 * l)
        lam_params = [t[l].reshape(1, HEAD_DIM) for t in (lam_q1, lam_k1, lam_q2, lam_k2)]

        z, dk, dv = _project(xp, g_mix_pre[l], w_in_bf, l, tabs_p, tm=tm_p, tn=SECTION, rope_sections=rope_secs,
                             k_section=1, n_main=5)
        ro, s_fin = _retention_prompt(z, ret_norm_g[l], B, S, rows_per_step=tm_p)
        do, *kv_stacks = _diff_prompt(z, dk, dv, lam_params, diff_norm_g[l], lam_init, B, S, l, depth, kv_stacks,
                                      tile=tile_attn, heads=N_DIFF_HEADS)
        xp = _merge(xp, z, ro, do, w_out_bf, l, g_mix_post[l], tm=tm_p)
        outs["p_ret"].append(s_fin)

        zs, dqs, dks, dvs = _project(xs, g_mix_pre[l], w_in_bf, l, tabs_s, tm=ms, tn=SECTION,
                                     rope_sections=rope_secs, k_section=1, n_main=4)

        def heads_first(t):
            t = t.reshape(DB, DS, N_RET_HEADS, HEAD_DIM).transpose(0, 2, 1, 3)
            return jnp.pad(t, ((0, 0), (0, 0), (0, ret_pad - DS), (0, 0)))

        ro_s, s_ret = _retention_sample(heads_first(zs[:, :SECTION]), heads_first(zs[:, SECTION:2 * SECTION]),
                                        heads_first(zs[:, 2 * SECTION:3 * SECTION]), state_ret[l], l, depth, s_ret,
                                        ret_norm_g[l],
                                        DS, bb=8)
        ro_s = ro_s[:, :, :DS].transpose(0, 2, 1, 3).reshape(ms, SECTION)

        g_rows = jnp.tile(diff_norm_g[l], (DS, 1))
        token_head_rows = lambda t: t.reshape(DB, DS * N_DIFF_HEADS, DIFF_DV)
        do_s = _diff_sample(token_head_rows(dqs), token_head_rows(dks), token_head_rows(dvs), kc, vc, page_table, l,
                            lam_params, g_rows, lam_init)
        xs = _merge(xs, zs, ro_s, do_s.reshape(ms, N_DIFF_HEADS * DIFF_DV), w_out_bf, l, g_mix_post[l], tm=ms)
        outs["s_dk"].append(dks.reshape(DB, DS, N_DIFF_HEADS, DIFF_DV))
        outs["s_dv"].append(dvs.reshape(DB, DS, N_DIFF_HEADS, DIFF_DV))

        mk, mv = _project(mem2d, g_mem_kv[l], w_mkv_bf, l, no_tabs_mem, tm=B * n_mem, tn=D, n_main=1)
        mk = mk.reshape(B, n_mem, D)
        mv = mv.reshape(B, n_mem, D)
        xp = _mem_prompt(xp, g_mem_pre[l], w_mq_bf, mk, mv, w_mo_bf, l, g_mem_post[l], S, tm=tm_p)
        qs, = _project(xs, g_mem_pre[l], w_mq_bf, l, no_tabs_s, tm=ms, tn=SECTION)
        q_rows = qs.reshape(DB, DS * N_MEM_HEADS, 2, LANES).transpose(0, 2, 1, 3)
        om = _mem_sample(q_rows.reshape(DB, 2 * DS * N_MEM_HEADS, LANES), mkc, mvc, l, bb=4)
        xs = _out_proj(xs, om.reshape(ms, D), w_mo_bf, l, g_mem_post[l], tm=ms)
        outs["p_mk"].append(mk.reshape(B, n_mem, N_MEM_HEADS, mem_hd))
        outs["p_mv"].append(mv.reshape(B, n_mem, N_MEM_HEADS, mem_hd))

        xp, tails = _conv_ffn_rows(xp, g_ffn_pre[l], w_up_bf, conv_w[l], conv_b[l], w_down_bf, l, zero_hist,
                                   g_ffn_post[l], tm=tm_p, tf=tf, blocks_per_seq=S // tm_p)
        outs["p_conv"].append(tails[jnp.arange(B) * (S // tm_p) + (S // tm_p - 1)])
        xs_t = xs.reshape(DB, DS, D).transpose(1, 0, 2).reshape(ms, D)
        hist_s = state_conv[l].transpose(1, 0, 2).reshape(1, 2 * DB, two_f)
        xs_t, tg, tv = _conv_ffn(xs_t, g_ffn_pre[l], w_up_bf, conv_w[l], conv_b[l], w_down_bf, l, hist_s,
                                 g_ffn_post[l], tm=ms, tf=tf, shift=DB, blocks_per_seq=1)
        xs = xs_t.reshape(DS, DB, D).transpose(1, 0, 2).reshape(ms, D)
        outs["s_conv"].append(jnp.concatenate([tg[0], tv[0]], axis=-1).reshape(2, DB, two_f).transpose(1, 0, 2))

    st = lambda k: jnp.stack(outs[k])
    p_dk, p_dv = (t.reshape(depth, B, S, N_DIFF_HEADS, DIFF_DV) for t in kv_stacks)
    return (xp.reshape(B, S, D), xs.reshape(DB, DS, D), st("p_ret"), p_dk, p_dv, st("p_mk"), st("p_mv"),
            st("p_conv"), s_ret, st("s_dk"), st("s_dv"), st("s_conv"))
```
